```python
import jax, jax.numpy as jnp
from jax import lax
import numpy as np

D_MODEL = 1024
BATCH = 8
SEQ = 8192
DEPTH = 2

N_HEADS = 16
HEAD_DIM = D_MODEL // N_HEADS
N_MIXERS = 2
DILATED_GROUPS = ((128, 1), (512, 4), (2048, 16))
N_DIL_GROUPS = len(DILATED_GROUPS)
N_KV_HEADS = 4
Q_PER_KV = N_HEADS // N_KV_HEADS
Q_BLOCK = 128
GRID_W = 64
ROPE_THETA = 10000.0
D_FF = (7 * D_MODEL) // 2
N_EXPERTS = 8
TOP_K = 2
N_MOD = 6
EPS = 1e-6
NEG_INF = -1e30

kernel_name = 'hybrid_dilated_axial_moe_encoder'


def rms_norm(x, g):
    xf = x.astype(jnp.float32)
    y = xf * lax.rsqrt(jnp.mean(xf * xf, axis=-1, keepdims=True) + EPS)
    return (y * g.astype(jnp.float32)).astype(x.dtype)


def modulate(x, g, shift, scale):
    return rms_norm(x, g) * (1 + scale[:, None, :]) + shift[:, None, :]


def alibi_slopes(n):
    return jnp.exp2(-8.0 * jnp.arange(1, n + 1, dtype=jnp.float32) / n)


def dilated_window_attention(q, k, v, window, dilation, slopes):
    B, T, H, dh = q.shape
    n_side = window // (2 * dilation)
    blk = n_side
    L = T // dilation
    nb = -(-L // blk)
    Lp = nb * blk

    def split(a):
        a = a.reshape(B, L, dilation, H, dh).transpose(0, 2, 3, 1, 4)
        return jnp.pad(a, ((0, 0), (0, 0), (0, 0), (0, Lp - L), (0, 0)))

    def band(a):
        ap = jnp.pad(a, ((0, 0), (0, 0), (0, 0), (blk, blk), (0, 0)))
        ap = ap.reshape(B, dilation, H, nb + 2, blk, dh)
        return jnp.concatenate([ap[:, :, :, :-2], ap[:, :, :, 1:-1], ap[:, :, :, 2:]], axis=4)

    qb = split(q).reshape(B, dilation, H, nb, blk, dh)
    kb = band(split(k))
    vb = band(split(v))
    a = jnp.arange(blk)
    b = jnp.arange(3 * blk)
    i = jnp.arange(nb)
    rel = b[None, :] - blk - a[:, None]
    u_key = (i[:, None] - 1) * blk + b[None, :]
    valid = (jnp.abs(rel) <= n_side)[None] & ((u_key >= 0) & (u_key < L))[:, None, :]
    bias = -slopes[:, None, None] * (dilation * jnp.abs(rel)).astype(jnp.float32)
    s = jnp.einsum('brhnqd,brhnkd->brhnqk', qb, kb, preferred_element_type=jnp.float32) * (dh ** -0.5)
    s = jnp.where(valid, s + bias[:, None], NEG_INF)
    m = jnp.max(s, axis=-1, keepdims=True)
    p = jnp.exp(s - m)
    den = jnp.sum(p, axis=-1)
    o = jnp.einsum('brhnqk,brhnkd->brhnqd', p.astype(vb.dtype), vb,
                   preferred_element_type=jnp.float32) / den[..., None]
    lse = m[..., 0] + jnp.log(den)
    o = o.reshape(B, dilation, H, Lp, dh)[:, :, :, :L].transpose(0, 3, 1, 2, 4).reshape(B, T, H, dh)
    lse = lse.reshape(B, dilation, H, Lp)[:, :, :, :L].transpose(0, 3, 1, 2).reshape(B, T, H)
    return o, lse


def dilated_mixer(h, w_in, w_out):
    B, T, D = h.shape
    slopes = alibi_slopes(N_HEADS)
    outs, lses = [], []
    for g, (window, dilation) in enumerate(DILATED_GROUPS):
        qkv = (h @ w_in[:, g * 3 * D:(g + 1) * 3 * D]).reshape(B, T, 3, N_HEADS, HEAD_DIM)
        o, lse = dilated_window_attention(qkv[:, :, 0], qkv[:, :, 1], qkv[:, :, 2], window, dilation, slopes)
        outs.append(o)
        lses.append(lse)
    wts = jax.nn.softmax(jnp.stack(lses, axis=0), axis=0)
    o = jnp.sum(wts[..., None] * jnp.stack(outs, axis=0), axis=0)
    return o.reshape(B, T, D).astype(h.dtype) @ w_out


def rope_1d(x, pos):
    half = x.shape[-1] // 2
    freqs = ROPE_THETA ** (-jnp.arange(half, dtype=jnp.float32) / half)
    ang = pos[:, None] * freqs[None, :]
    cos = jnp.cos(ang)[None, :, None, :]
    sin = jnp.sin(ang)[None, :, None, :]
    x1, x2 = x[..., :half], x[..., half:]
    return jnp.concatenate([x1 * cos - x2 * sin, x1 * sin + x2 * cos], axis=-1)


def axial_rope(x):
    B, T, H, dh = x.shape
    rows = T // GRID_W
    row = jnp.broadcast_to(jnp.arange(rows, dtype=jnp.float32)[:, None], (rows, GRID_W)).reshape(T)
    col = jnp.broadcast_to(jnp.arange(GRID_W, dtype=jnp.float32)[None, :], (rows, GRID_W)).reshape(T)
    xf = x.astype(jnp.float32)
    r = dh // 2
    return jnp.concatenate([rope_1d(xf[..., :r], row), rope_1d(xf[..., r:], col)], axis=-1).astype(x.dtype)


def axial_gqa_mixer(h, w_in, q_norm_g, k_norm_g, w_out):
    B, T, D = h.shape
    kv_w = N_KV_HEADS * HEAD_DIM
    proj = h @ w_in
    q = proj[..., :D].reshape(B, T, N_HEADS, HEAD_DIM)
    k = proj[..., D:D + kv_w].reshape(B, T, N_KV_HEADS, HEAD_DIM)
    v = proj[..., D + kv_w:].reshape(B, T, N_KV_HEADS, HEAD_DIM)
    q = axial_rope(rms_norm(q, q_norm_g))
    k = axial_rope(rms_norm(k, k_norm_g))
    nq = T // Q_BLOCK
    qb = q.reshape(B, nq, Q_BLOCK, N_KV_HEADS, Q_PER_KV, HEAD_DIM).transpose(1, 0, 2, 3, 4, 5)
    scale = HEAD_DIM ** -0.5

    def attend(q_blk):
        s = jnp.einsum('bqkgd,bskd->bkgqs', q_blk, k, preferred_element_type=jnp.float32) * scale
        p = jax.nn.softmax(s, axis=-1)
        return jnp.einsum('bkgqs,bskd->bqkgd', p.astype(v.dtype), v)

    o = lax.map(attend, qb)
    o = o.transpose(1, 0, 2, 3, 4, 5).reshape(B, T, D)
    return o @ w_out


def swiglu(h, w_gate, w_up, w_down):
    return (jax.nn.silu(h @ w_gate) * (h @ w_up)) @ w_down


def moe_swiglu(h, router_w, w_gate, w_up, w_down):
    logits = (h @ router_w).astype(jnp.float32)
    top_val, top_idx = lax.top_k(logits, TOP_K)
    top_w = jax.nn.softmax(top_val, axis=-1)
    gates = jnp.sum(jax.nn.one_hot(top_idx, N_EXPERTS, dtype=jnp.float32) * top_w[..., None], axis=-2)
    gates = gates.astype(h.dtype)
    y = jnp.zeros_like(h)
    for e in range(N_EXPERTS):
        y = y + gates[..., e:e + 1] * swiglu(h, w_gate[e], w_up[e], w_down[e])
    return y


def setup_inputs(seed: int = 0) -> dict:
    key = jax.random.key(seed)
    ks = jax.random.split(key, 18)
    f32 = jnp.float32
    D = D_MODEL

    def nrm(k, shape, fan_in):
        return jax.random.normal(k, shape, f32) * (fan_in ** -0.5)

    return {
        'x': jax.random.normal(ks[0], (BATCH, SEQ, D), f32),
        'c': jax.random.normal(ks[1], (BATCH, D), f32),
        'mod_w': nrm(ks[2], (DEPTH, D, N_MOD * D), D),
        'mod_b': 0.02 * jax.random.normal(ks[3], (DEPTH, N_MOD * D), f32),
        'norm_g': 1.0 + 0.02 * jax.random.normal(ks[4], (DEPTH, 4, D), f32),
        'l0_w_in': nrm(ks[5], (D, N_DIL_GROUPS * 3 * D), D),
        'l0_w_out': nrm(ks[6], (D, D), D),
        'l0_ffn_w_gate': nrm(ks[7], (D, D_FF), D),
        'l0_ffn_w_up': nrm(ks[8], (D, D_FF), D),
        'l0_ffn_w_down': nrm(ks[9], (D_FF, D), D_FF),
        'l1_w_in': nrm(ks[10], (D, D + 2 * N_KV_HEADS * HEAD_DIM), D),
        'l1_q_norm_g': 1.0 + 0.02 * jax.random.normal(ks[11], (HEAD_DIM,), f32),
        'l1_k_norm_g': 1.0 + 0.02 * jax.random.normal(ks[12], (HEAD_DIM,), f32),
        'l1_w_out': nrm(ks[13], (D, D), D),
        'l1_router_w': nrm(ks[14], (D, N_EXPERTS), D),
        'l1_exp_w_gate': nrm(ks[15], (N_EXPERTS, D, D_FF), D),
        'l1_exp_w_up': nrm(ks[16], (N_EXPERTS, D, D_FF), D),
        'l1_exp_w_down': nrm(ks[17], (N_EXPERTS, D_FF, D), D_FF),
    }


def reference(x, c, mod_w, mod_b, norm_g, l0_w_in, l0_w_out, l0_ffn_w_gate, l0_ffn_w_up, l0_ffn_w_down,
              l1_w_in, l1_q_norm_g, l1_k_norm_g, l1_w_out, l1_router_w, l1_exp_w_gate, l1_exp_w_up,
              l1_exp_w_down):
    mixers = (
        lambda h: dilated_mixer(h, l0_w_in, l0_w_out),
        lambda h: axial_gqa_mixer(h, l1_w_in, l1_q_norm_g, l1_k_norm_g, l1_w_out),
    )
    channel_mixers = (
        lambda h: swiglu(h, l0_ffn_w_gate, l0_ffn_w_up, l0_ffn_w_down),
        lambda h: moe_swiglu(h, l1_router_w, l1_exp_w_gate, l1_exp_w_up, l1_exp_w_down),
    )
    c_act = jax.nn.silu(c)
    for i in range(DEPTH):
        mod = c_act @ mod_w[i] + mod_b[i]
        sh_a, sc_a, g_a, sh_f, sc_f, g_f = jnp.split(mod, N_MOD, axis=-1)
        y = mixers[i % N_MIXERS](modulate(x, norm_g[i, 0], sh_a, sc_a))
        x = x + g_a[:, None, :] * rms_norm(y, norm_g[i, 1])
        y = channel_mixers[i % 2](modulate(x, norm_g[i, 2], sh_f, sc_f))
        x = x + g_f[:, None, :] * rms_norm(y, norm_g[i, 3])
    return x
```

```python
import functools
import math

import numpy as np
import jax
import jax.numpy as jnp
from jax import lax
from jax.experimental import pallas as pl
from jax.experimental.pallas import tpu as pltpu

F32 = jnp.float32
BF16 = jnp.bfloat16

D_MODEL = 1024
N_HEADS = 16
HEAD_DIM = 64
LANES = 128
N_PAIRS = D_MODEL // LANES
DILATED_GROUPS = ((128, 1), (512, 4), (2048, 16))
N_SIDE = 64
N_KV_HEADS = 4
Q_PER_KV = N_HEADS // N_KV_HEADS
GRID_W = 64
ROPE_THETA = 10000.0
D_FF = 3584
N_EXPERTS = 8
N_MOD = 6
EPS = 1e-6
NEG_BIG = -1e30
VMEM_LIMIT_BYTES = 56 * 1024 * 1024


def _params(n_axes, n_arbitrary=0):
    sem = ("parallel",) * (n_axes - n_arbitrary) + ("arbitrary",) * n_arbitrary
    return pltpu.CompilerParams(dimension_semantics=sem, vmem_limit_bytes=VMEM_LIMIT_BYTES)


def _resident(shape, index_map):
    return pl.BlockSpec(shape, index_map, pipeline_mode=pl.Buffered(1))


def _rms(x, g):
    ms = jnp.mean(x * x, axis=-1, keepdims=True)
    return x * lax.rsqrt(ms + EPS) * g


def _prenorm(x, g, shift, scale):
    return _rms(x, g) * (1.0 + scale) + shift


def _sigmoid(x):
    return 1.0 / (1.0 + jnp.exp(-x))


def _mod_kernel(c_ref, w_ref, b_ref, o_ref):
    c = c_ref[...]
    ca = c * _sigmoid(c)
    o_ref[0] = jnp.dot(ca, w_ref[0], preferred_element_type=F32, precision=lax.Precision.HIGHEST) + b_ref[0]


def _mod_vectors(c, mod_w, mod_b):
    depth, d, n6 = mod_w.shape
    b = c.shape[0]
    tn = 1536
    return pl.pallas_call(
        _mod_kernel,
        grid=(depth, n6 // tn),
        in_specs=[
            pl.BlockSpec((b, d), lambda l, j: (0, 0)),
            pl.BlockSpec((1, d, tn), lambda l, j: (l, 0, j)),
            pl.BlockSpec((1, 1, tn), lambda l, j: (l, 0, j)),
        ],
        out_specs=pl.BlockSpec((1, b, tn), lambda l, j: (l, 0, j)),
        out_shape=jax.ShapeDtypeStruct((depth, b, n6), F32),
        compiler_params=_params(2),
        name="mod_vectors",
    )(c, mod_w, mod_b.reshape(depth, 1, n6))


PERM_SUB = 256


def _perm_matrix(d):
    p = np.zeros((PERM_SUB, PERM_SUB), np.float32)
    j = np.arange(PERM_SUB)
    p[(j % d) * (PERM_SUB // d) + j // d, j] = 1.0
    return p


def _proj0_kernel(x_ref, g_ref, sh_ref, sc_ref, p_ref, w_ref, o_ref, h_scr, *, d, tm, cw):
    h = _prenorm(x_ref[...], g_ref[...], sh_ref[...], sc_ref[...]).astype(BF16)
    per = tm // d
    if d == 1:
        h_scr[...] = h
    else:
        sub_per = PERM_SUB // d
        for s in range(tm // PERM_SUB):
            hs = jnp.dot(p_ref[...], h[s * PERM_SUB:(s + 1) * PERM_SUB], preferred_element_type=F32).astype(BF16)
            for r in range(d):
                h_scr[r * per + s * sub_per:r * per + (s + 1) * sub_per, :] = hs[r * sub_per:(r + 1) * sub_per]
    hp = h_scr[...]
    for j in range(w_ref.shape[1] // cw):
        res = jnp.dot(hp, w_ref[:, j * cw:(j + 1) * cw], preferred_element_type=F32).astype(BF16)
        for r in range(d):
            o_ref[r, :, j * cw:(j + 1) * cw] = res[r * per:(r + 1) * per]


def _proj0(x2d, mod_r, mod_base, g, w, d, batch, seq):
    n = x2d.shape[0]
    tm = 512
    tiles_per_seq = seq // tm
    ncol = w.shape[1]
    kern = functools.partial(_proj0_kernel, d=d, tm=tm, cw=512)
    perm = jnp.asarray(_perm_matrix(d), BF16)
    return pl.pallas_call(
        kern,
        grid=(n // tm,),
        in_specs=[
            pl.BlockSpec((tm, D_MODEL), lambda i: (i, 0)),
            _resident((1, D_MODEL), lambda i: (0, 0)),
            pl.BlockSpec((None, 1, D_MODEL), lambda i: (mod_base + (i // tiles_per_seq) * N_MOD + 0, 0, 0)),
            pl.BlockSpec((None, 1, D_MODEL), lambda i: (mod_base + (i // tiles_per_seq) * N_MOD + 1, 0, 0)),
            _resident((PERM_SUB, PERM_SUB), lambda i: (0, 0)),
            _resident((D_MODEL, ncol), lambda i: (0, 0)),
        ],
        out_specs=pl.BlockSpec((None, d, tm // d, ncol), lambda i: (i // tiles_per_seq, 0, i % tiles_per_seq, 0)),
        out_shape=jax.ShapeDtypeStruct((batch, d, seq // d, ncol), BF16),
        scratch_shapes=[pltpu.VMEM((tm, D_MODEL), BF16)],
        compiler_params=_params(1),
        name=f"l0_qkv_d{d}",
    )(x2d, g, mod_r, mod_r, perm, w)


ATT0_SQ = 128
ATT0_TK = ATT0_SQ + 2 * N_SIDE


def _attn0_kernel(q_ref, kp_ref, kc_ref, kn_ref, vp_ref, vc_ref, vn_ref, o_ref, lse_ref, kbuf, vbuf,
                  *, d, tq, length):
    i = pl.program_id(2)
    kbuf[0:N_SIDE] = kp_ref[...]
    kbuf[N_SIDE:N_SIDE + tq] = kc_ref[...]
    kbuf[N_SIDE + tq:] = kn_ref[...]
    vbuf[0:N_SIDE] = vp_ref[...]
    vbuf[N_SIDE:N_SIDE + tq] = vc_ref[...]
    vbuf[N_SIDE + tq:] = vn_ref[...]

    lane = lax.broadcasted_iota(jnp.int32, (ATT0_SQ, LANES), 1)
    low = lane < HEAD_DIM
    a_idx = lax.broadcasted_iota(jnp.int32, (ATT0_SQ, ATT0_TK), 0)
    c_idx = lax.broadcasted_iota(jnp.int32, (ATT0_SQ, ATT0_TK), 1)
    absrel = jnp.abs(c_idx - N_SIDE - a_idx)
    for j in range(tq // ATT0_SQ):
        u_key = i * tq + (j * ATT0_SQ - N_SIDE) + c_idx
        valid = (absrel <= N_SIDE) & (u_key >= 0) & (u_key < length)
        base = jnp.where(valid, (-float(d)) * absrel.astype(F32), NEG_BIG)
        lse_tile = jnp.zeros((ATT0_SQ, LANES), F32)
        for p in range(N_PAIRS):
            cols = slice(p * LANES, (p + 1) * LANES)
            q2 = q_ref[j * ATT0_SQ:(j + 1) * ATT0_SQ, cols]
            zero = jnp.zeros_like(q2)
            qs = jnp.concatenate([jnp.where(low, q2, zero), jnp.where(low, zero, q2)], axis=0)
            k2 = kbuf[j * ATT0_SQ:j * ATT0_SQ + ATT0_TK, cols]
            v2 = vbuf[j * ATT0_SQ:j * ATT0_SQ + ATT0_TK, cols]
            s = lax.dot_general(qs, k2, (((1,), (1,)), ((), ())), preferred_element_type=F32)
            s = s * (HEAD_DIM ** -0.5)
            ps, ls = [], []
            for hh in range(2):
                head = 2 * p + hh
                slope = 2.0 ** (-8.0 * (head + 1) / N_HEADS)
                sh = s[hh * ATT0_SQ:(hh + 1) * ATT0_SQ] + slope * base
                m = jnp.max(sh, axis=-1, keepdims=True)
                e = jnp.exp(sh - m)
                l = jnp.sum(e, axis=-1, keepdims=True)
                ps.append(e.astype(BF16))
                ls.append(l)
                lse_tile = jnp.where(lane == head, m + jnp.log(l), lse_tile)
            o = jnp.dot(jnp.concatenate(ps, axis=0), v2, preferred_element_type=F32)
            o2 = jnp.where(low, o[:ATT0_SQ] / ls[0], o[ATT0_SQ:] / ls[1])
            o_ref[j * ATT0_SQ:(j + 1) * ATT0_SQ, cols] = o2.astype(BF16)
        lse_ref[j * ATT0_SQ:(j + 1) * ATT0_SQ, :] = lse_tile


def _attn0(qkv, d):
    batch, _, length, _ = qkv.shape
    tq = min(512, length)
    nb = tq // N_SIDE
    last = length // N_SIDE - 1
    kern = functools.partial(_attn0_kernel, d=d, tq=tq, length=length)

    def main(col):
        return pl.BlockSpec((None, None, tq, D_MODEL), lambda b, r, i: (b, r, i, col))

    def prev(col):
        return pl.BlockSpec((None, None, N_SIDE, D_MODEL), lambda b, r, i: (b, r, jnp.maximum(i * nb - 1, 0), col))

    def nxt(col):
        return pl.BlockSpec((None, None, N_SIDE, D_MODEL), lambda b, r, i: (b, r, jnp.minimum((i + 1) * nb, last), col))

    return pl.pallas_call(
        kern,
        grid=(batch, d, length // tq),
        in_specs=[main(0), prev(1), main(1), nxt(1), prev(2), main(2), nxt(2)],
        out_specs=[
            pl.BlockSpec((None, None, tq, D_MODEL), lambda b, r, i: (b, r, i, 0)),
            pl.BlockSpec((None, None, tq, LANES), lambda b, r, i: (b, r, i, 0)),
        ],
        out_shape=[
            jax.ShapeDtypeStruct((batch, d, length, D_MODEL), BF16),
            jax.ShapeDtypeStruct((batch, d, length, LANES), F32),
        ],
        scratch_shapes=[
            pltpu.VMEM((tq + 2 * N_SIDE, D_MODEL), BF16),
            pltpu.VMEM((tq + 2 * N_SIDE, D_MODEL), BF16),
        ],
        compiler_params=_params(3),
        name=f"l0_attn_d{d}",
    )(qkv, qkv, qkv, qkv, qkv, qkv, qkv)


def _split3(x):
    hi = x.astype(BF16)
    r1 = x - hi.astype(F32)
    mid = r1.astype(BF16)
    lo = (r1 - mid.astype(F32)).astype(BF16)
    return hi, mid, lo


def _unpermute(pt, blk_ref, d, tm, exact_f32):
    sub_per = PERM_SUB // d
    outs = []
    for s in range(tm // PERM_SUB):
        src = jnp.concatenate([blk_ref[r, s * sub_per:(s + 1) * sub_per, :] for r in range(d)], axis=0)
        if exact_f32:
            parts = _split3(src)
            nat = sum(jnp.dot(pt, part, preferred_element_type=F32) for part in parts)
        else:
            nat = jnp.dot(pt, src, preferred_element_type=F32)
        outs.append(nat)
    return jnp.concatenate(outs, axis=0)


def _merge_out_kernel(o0_ref, o1_ref, o2_ref, l0_ref, l1_ref, l2_ref, p1_ref, p2_ref, e_ref,
                      x_ref, w_ref, g_ref, gate_ref, out_ref, *, tm):
    dils = [dil for _, dil in DILATED_GROUPS]
    o_refs = (o0_ref, o1_ref, o2_ref)
    l_refs = (l0_ref, l1_ref, l2_ref)
    pts = (None, p1_ref[...], p2_ref[...])
    os_, ls_ = [], []
    for gi, d in enumerate(dils):
        if d == 1:
            os_.append(o_refs[gi][0].astype(F32))
            ls_.append(l_refs[gi][0])
        else:
            os_.append(_unpermute(pts[gi], o_refs[gi], d, tm, False))
            ls_.append(_unpermute(pts[gi], l_refs[gi], d, tm, True))
    m = jnp.maximum(jnp.maximum(ls_[0], ls_[1]), ls_[2])
    es = [jnp.exp(l - m) for l in ls_]
    den = es[0] + es[1] + es[2]
    acc = jnp.zeros((tm, D_MODEL), F32)
    for gi in range(3):
        wgt = es[gi] / den
        hi, mid, lo = _split3(wgt)
        wfull = (jnp.dot(hi, e_ref[...], preferred_element_type=F32)
                 + jnp.dot(mid, e_ref[...], preferred_element_type=F32)
                 + jnp.dot(lo, e_ref[...], preferred_element_type=F32))
        acc = acc + wfull * os_[gi]
    y = jnp.dot(acc.astype(BF16), w_ref[...], preferred_element_type=F32)
    out_ref[...] = x_ref[...] + gate_ref[...] * _rms(y, g_ref[...])


def _merge_out(o_list, lse_list, x2d, mod_r, mod_base, w_out, g, batch, seq):
    n = x2d.shape[0]
    tm = 512
    tiles_per_seq = seq // tm
    kern = functools.partial(_merge_out_kernel, tm=tm)
    dils = [dil for _, dil in DILATED_GROUPS]
    expand = np.zeros((LANES, D_MODEL), np.float32)
    for h in range(N_HEADS):
        expand[h, h * HEAD_DIM:(h + 1) * HEAD_DIM] = 1.0

    def grouped(d, w):
        return pl.BlockSpec((None, d, tm // d, w), lambda i: (i // tiles_per_seq, 0, i % tiles_per_seq, 0))

    in_specs = ([grouped(d, D_MODEL) for d in dils] + [grouped(d, LANES) for d in dils] + [
        _resident((PERM_SUB, PERM_SUB), lambda i: (0, 0)),
        _resident((PERM_SUB, PERM_SUB), lambda i: (0, 0)),
        _resident((LANES, D_MODEL), lambda i: (0, 0)),
        pl.BlockSpec((tm, D_MODEL), lambda i: (i, 0)),
        _resident((D_MODEL, D_MODEL), lambda i: (0, 0)),
        _resident((1, D_MODEL), lambda i: (0, 0)),
        pl.BlockSpec((None, 1, D_MODEL), lambda i: (mod_base + (i // tiles_per_seq) * N_MOD + 2, 0, 0)),
    ])
    return pl.pallas_call(
        kern,
        grid=(n // tm,),
        in_specs=in_specs,
        out_specs=pl.BlockSpec((tm, D_MODEL), lambda i: (i, 0)),
        out_shape=jax.ShapeDtypeStruct((n, D_MODEL), F32),
        compiler_params=_params(1),
        name="l0_merge_out",
    )(*o_list, *lse_list,
      jnp.asarray(_perm_matrix(dils[1]).T, BF16), jnp.asarray(_perm_matrix(dils[2]).T, BF16),
      jnp.asarray(expand, BF16), x2d, w_out, g, mod_r)


def _out_kernel(o_ref, x_ref, w_ref, g_ref, gate_ref, out_ref):
    y = jnp.dot(o_ref[...], w_ref[...], preferred_element_type=F32)
    out_ref[...] = x_ref[...] + gate_ref[...] * _rms(y, g_ref[...])


def _out_proj(o2d, x2d, mod_r, mod_base, w_out, g, seq):
    n = x2d.shape[0]
    tm = 512
    tiles_per_seq = seq // tm
    return pl.pallas_call(
        _out_kernel,
        grid=(n // tm,),
        in_specs=[
            pl.BlockSpec((tm, D_MODEL), lambda i: (i, 0)),
            pl.BlockSpec((tm, D_MODEL), lambda i: (i, 0)),
            _resident((D_MODEL, D_MODEL), lambda i: (0, 0)),
            _resident((1, D_MODEL), lambda i: (0, 0)),
            pl.BlockSpec((None, 1, D_MODEL), lambda i: (mod_base + (i // tiles_per_seq) * N_MOD + 2, 0, 0)),
        ],
        out_specs=pl.BlockSpec((tm, D_MODEL), lambda i: (i, 0)),
        out_shape=jax.ShapeDtypeStruct((n, D_MODEL), F32),
        compiler_params=_params(1),
        name="l1_out_proj",
    )(o2d, x2d, w_out, g, mod_r)


FF_CHUNK = 512


def _swiglu_acc(h, wg_ref, wu_ref, wd_ref, acc):
    for c in range(wg_ref.shape[-1] // FF_CHUNK):
        cols = slice(c * FF_CHUNK, (c + 1) * FF_CHUNK)
        a = jnp.dot(h, wg_ref[:, cols], preferred_element_type=F32)
        u = jnp.dot(h, wu_ref[:, cols], preferred_element_type=F32)
        t = (a * _sigmoid(a) * u).astype(BF16)
        acc = acc + jnp.dot(t, wd_ref[cols, :], preferred_element_type=F32)
    return acc


def _ffn_kernel(x_ref, g_ref, sh_ref, sc_ref, wg_ref, wu_ref, wd_ref, g2_ref, gate_ref, out_ref):
    x = x_ref[...]
    h = _prenorm(x, g_ref[...], sh_ref[...], sc_ref[...]).astype(BF16)
    y = _swiglu_acc(h, wg_ref, wu_ref, wd_ref, jnp.zeros(x.shape, F32))
    out_ref[...] = x + gate_ref[...] * _rms(y, g2_ref[...])


def _ffn(x2d, mod_r, mod_base, g_pre, g_post, wg, wu, wd, seq):
    n = x2d.shape[0]
    tm = 512
    tiles_per_seq = seq // tm

    def modspec(k):
        return pl.BlockSpec((None, 1, D_MODEL), lambda i: (mod_base + (i // tiles_per_seq) * N_MOD + k, 0, 0))

    return pl.pallas_call(
        _ffn_kernel,
        grid=(n // tm,),
        in_specs=[
            pl.BlockSpec((tm, D_MODEL), lambda i: (i, 0)),
            _resident((1, D_MODEL), lambda i: (0, 0)),
            modspec(3), modspec(4),
            _resident((D_MODEL, D_FF), lambda i: (0, 0)),
            _resident((D_MODEL, D_FF), lambda i: (0, 0)),
            _resident((D_FF, D_MODEL), lambda i: (0, 0)),
            _resident((1, D_MODEL), lambda i: (0, 0)),
            modspec(5),
        ],
        out_specs=pl.BlockSpec((tm, D_MODEL), lambda i: (i, 0)),
        out_shape=jax.ShapeDtypeStruct((n, D_MODEL), F32),
        compiler_params=_params(1),
        name="l0_ffn",
    )(x2d, g_pre, mod_r, mod_r, wg, wu, wd, g_post, mod_r)


def _proj1_kernel(x_ref, g_ref, sh_ref, sc_ref, w_ref, m2_ref, qg_ref, kg_ref, cos_ref, sn_ref, sp_ref,
                  q_ref, kd_ref, vd_ref):
    h = _prenorm(x_ref[...], g_ref[...], sh_ref[...], sc_ref[...]).astype(BF16)
    proj = jnp.dot(h, w_ref[...], preferred_element_type=F32)
    tm = proj.shape[0]
    lane = lax.broadcasted_iota(jnp.int32, (tm, LANES), 1)
    low = lane < HEAD_DIM
    cos, sn, sp = cos_ref[...], sn_ref[...], sp_ref[...]
    n_q = D_MODEL // LANES
    n_k = N_KV_HEADS * HEAD_DIM // LANES
    for c in range(n_q + n_k):
        z = proj[:, c * LANES:(c + 1) * LANES]
        zz = z * z
        hi = zz.astype(BF16)
        lo = (zz - hi.astype(F32)).astype(BF16)
        ms = jnp.dot(hi, m2_ref[...], preferred_element_type=F32) + jnp.dot(lo, m2_ref[...], preferred_element_type=F32)
        gain = qg_ref[...] if c < n_q else kg_ref[...]
        zn = z * lax.rsqrt(ms + EPS) * gain
        zr = zn * cos + pltpu.roll(zn, LANES - 16, 1) * sn + pltpu.roll(zn, 16, 1) * sp
        if c < n_q:
            q_ref[:, c * LANES:(c + 1) * LANES] = (zr * (HEAD_DIM ** -0.5)).astype(BF16)
        else:
            j = c - n_q
            sw = pltpu.roll(zr, HEAD_DIM, 1)
            kd_ref[:, (2 * j) * LANES:(2 * j + 1) * LANES] = jnp.where(low, zr, sw).astype(BF16)
            kd_ref[:, (2 * j + 1) * LANES:(2 * j + 2) * LANES] = jnp.where(low, sw, zr).astype(BF16)
    v_base = D_MODEL + N_KV_HEADS * HEAD_DIM
    ones = jnp.ones((tm, LANES), F32)
    for j in range(n_k):
        z = proj[:, v_base + j * LANES:v_base + (j + 1) * LANES]
        sw = pltpu.roll(z, HEAD_DIM, 1)
        vd_ref[:, (2 * j) * LANES:(2 * j + 1) * LANES] = jnp.where(low, z, ones).astype(BF16)
        vd_ref[:, (2 * j + 1) * LANES:(2 * j + 2) * LANES] = jnp.where(low, sw, ones).astype(BF16)


def _rope_tables(seq):
    lane = np.arange(LANES)
    dd = lane % HEAD_DIM
    blk = dd // (HEAD_DIM // 2)
    idx = dd % (HEAD_DIM // 2)
    half = HEAD_DIM // 4
    fi = idx % half
    first = idx < half
    freqs = jnp.asarray(ROPE_THETA, F32) ** (-jnp.arange(half, dtype=F32) / half)
    t = jnp.arange(seq)
    row = (t // GRID_W).astype(F32)
    col = (t % GRID_W).astype(F32)
    pos = jnp.where(jnp.asarray(blk == 0)[None, :], row[:, None], col[:, None])
    ang = pos * freqs[jnp.asarray(fi)][None, :]
    cos = jnp.cos(ang)
    sin = jnp.sin(ang)
    first = jnp.asarray(first)[None, :]
    return cos, jnp.where(first, -sin, 0.0), jnp.where(first, 0.0, sin)


def _proj1(x2d, mod_r, mod_base, g, w, qg, kg, seq):
    n = x2d.shape[0]
    tm = 512
    tiles_per_seq = seq // tm
    ncol = w.shape[1]
    kvw = N_KV_HEADS * LANES
    m2 = np.zeros((LANES, LANES), np.float32)
    m2[:HEAD_DIM, :HEAD_DIM] = 1.0 / HEAD_DIM
    m2[HEAD_DIM:, HEAD_DIM:] = 1.0 / HEAD_DIM
    cos, sn, sp = _rope_tables(seq)
    qg2 = jnp.tile(qg.astype(F32), 2).reshape(1, LANES)
    kg2 = jnp.tile(kg.astype(F32), 2).reshape(1, LANES)

    def modspec(k):
        return pl.BlockSpec((None, 1, D_MODEL), lambda i: (mod_base + (i // tiles_per_seq) * N_MOD + k, 0, 0))

    def table():
        return pl.BlockSpec((tm, LANES), lambda i: (i % tiles_per_seq, 0))

    return pl.pallas_call(
        _proj1_kernel,
        grid=(n // tm,),
        in_specs=[
            pl.BlockSpec((tm, D_MODEL), lambda i: (i, 0)),
            _resident((1, D_MODEL), lambda i: (0, 0)),
            modspec(0), modspec(1),
            _resident((D_MODEL, ncol), lambda i: (0, 0)),
            _resident((LANES, LANES), lambda i: (0, 0)),
            _resident((1, LANES), lambda i: (0, 0)),
            _resident((1, LANES), lambda i: (0, 0)),
            table(), table(), table(),
        ],
        out_specs=[
            pl.BlockSpec((tm, D_MODEL), lambda i: (i, 0)),
            pl.BlockSpec((tm, kvw), lambda i: (i, 0)),
            pl.BlockSpec((tm, kvw), lambda i: (i, 0)),
        ],
        out_shape=[
            jax.ShapeDtypeStruct((n, D_MODEL), BF16),
            jax.ShapeDtypeStruct((n, kvw), BF16),
            jax.ShapeDtypeStruct((n, kvw), BF16),
        ],
        compiler_params=_params(1),
        name="l1_qkv_rope",
    )(x2d, g, mod_r, mod_r, w, jnp.asarray(m2, BF16), qg2, kg2, cos, sn, sp)


def _attn1_kernel(q_ref, k_ref, v_ref, o_ref, qm_scr, m_scr, acc_scr, *, tq, tk, seq):
    lane = lax.broadcasted_iota(jnp.int32, (tq, LANES), 1)
    low = lane < HEAD_DIM
    for i4 in range(Q_PER_KV):
        grp = q_ref[:, (i4 // 2) * LANES:(i4 // 2 + 1) * LANES]
        keep = low if i4 % 2 == 0 else jnp.logical_not(low)
        qm_scr[i4 * tq:(i4 + 1) * tq, :] = jnp.where(keep, grp, jnp.zeros_like(grp))
    m_scr[...] = jnp.full(m_scr.shape, NEG_BIG, F32)
    acc_scr[...] = jnp.zeros(acc_scr.shape, F32)

    def body(kt, carry):
        start = pl.multiple_of(kt * tk, tk)
        k = k_ref[pl.ds(start, tk), :]
        v = v_ref[pl.ds(start, tk), :]
        s = lax.dot_general(qm_scr[...], k, (((1,), (1,)), ((), ())), preferred_element_type=F32)
        m_prev = m_scr[...]
        m_new = jnp.maximum(m_prev, jnp.max(s, axis=-1, keepdims=True))
        p = jnp.exp(s - m_new).astype(BF16)
        acc_scr[...] = jnp.exp(m_prev - m_new) * acc_scr[...] + jnp.dot(p, v, preferred_element_type=F32)
        m_scr[...] = m_new
        return carry

    lax.fori_loop(0, seq // tk, body, 0)
    lane_r = lax.broadcasted_iota(jnp.int32, (tq, LANES), 1)
    for g2 in range(Q_PER_KV // 2):
        halves = []
        for hh in range(2):
            a = acc_scr[(2 * g2 + hh) * tq:(2 * g2 + hh + 1) * tq, :]
            den = pltpu.roll(a, HEAD_DIM, 1)
            halves.append(a / den)
        o_ref[:, g2 * LANES:(g2 + 1) * LANES] = jnp.where(
            lane_r < HEAD_DIM, halves[0], pltpu.roll(halves[1], HEAD_DIM, 1)).astype(BF16)


def _attn1(q, kd, vd, batch, seq):
    n = q.shape[0]
    tq, tk = 512, 512
    qt = seq // tq
    qw = Q_PER_KV * HEAD_DIM
    kern = functools.partial(_attn1_kernel, tq=tq, tk=tk, seq=seq)
    return pl.pallas_call(
        kern,
        grid=(batch, N_KV_HEADS, qt),
        in_specs=[
            pl.BlockSpec((tq, qw), lambda b, j, i: (b * qt + i, j)),
            pl.BlockSpec((seq, LANES), lambda b, j, i: (b, j)),
            pl.BlockSpec((seq, LANES), lambda b, j, i: (b, j)),
        ],
        out_specs=pl.BlockSpec((tq, qw), lambda b, j, i: (b * qt + i, j)),
        out_shape=jax.ShapeDtypeStruct((n, D_MODEL), BF16),
        scratch_shapes=[
            pltpu.VMEM((Q_PER_KV * tq, LANES), BF16),
            pltpu.VMEM((Q_PER_KV * tq, 1), F32),
            pltpu.VMEM((Q_PER_KV * tq, LANES), F32),
        ],
        compiler_params=_params(3),
        name="l1_attn",
    )(q, kd, vd)


def _moe_kernel(x_ref, g_ref, sh_ref, sc_ref, rw_ref, wg_ref, wu_ref, wd_ref, g2_ref, gate_ref, out_ref,
                h_scr, gates_scr, acc_scr, *, n_chunks):
    e = pl.program_id(1)
    c = pl.program_id(2)
    tm = x_ref.shape[0]
    lane = lax.broadcasted_iota(jnp.int32, (tm, LANES), 1)

    @pl.when((e == 0) & (c == 0))
    def _route():
        h = _prenorm(x_ref[...], g_ref[...], sh_ref[...], sc_ref[...])
        h_scr[...] = h.astype(BF16)
        logits = jnp.dot(h, rw_ref[...], preferred_element_type=F32, precision=lax.Precision.HIGHEST)
        logits = jnp.where(lane < N_EXPERTS, logits, NEG_BIG)
        m1 = jnp.max(logits, axis=-1, keepdims=True)
        i1 = jnp.min(jnp.where(logits == m1, lane, LANES), axis=-1, keepdims=True)
        rest = jnp.where(lane == i1, NEG_BIG, logits)
        m2 = jnp.max(rest, axis=-1, keepdims=True)
        i2 = jnp.min(jnp.where(rest == m2, lane, LANES), axis=-1, keepdims=True)
        t = jnp.exp(m2 - m1)
        w1 = 1.0 / (1.0 + t)
        w2 = t / (1.0 + t)
        gates_scr[...] = jnp.where(lane == i1, w1, 0.0) + jnp.where(lane == i2, w2, 0.0)
        acc_scr[...] = jnp.zeros(acc_scr.shape, F32)

    h = h_scr[...]
    a = jnp.dot(h, wg_ref[0], preferred_element_type=F32)
    u = jnp.dot(h, wu_ref[0], preferred_element_type=F32)
    t = (a * _sigmoid(a) * u).astype(BF16)
    gate_col = jnp.sum(jnp.where(lane == e, gates_scr[...], 0.0), axis=-1, keepdims=True)
    acc_scr[...] += gate_col * jnp.dot(t, wd_ref[0], preferred_element_type=F32)

    @pl.when((e == N_EXPERTS - 1) & (c == n_chunks - 1))
    def _finish():
        out_ref[...] = x_ref[...] + gate_ref[...] * _rms(acc_scr[...], g2_ref[...])


def _moe(x2d, mod_r, mod_base, g_pre, g_post, router_w, wg, wu, wd, seq):
    n = x2d.shape[0]
    tm = 1024
    tiles_per_seq = seq // tm
    n_chunks = D_FF // FF_CHUNK
    rw = jnp.zeros((D_MODEL, LANES), F32).at[:, :N_EXPERTS].set(router_w.astype(F32))
    kern = functools.partial(_moe_kernel, n_chunks=n_chunks)

    def modspec(k):
        return pl.BlockSpec((None, 1, D_MODEL), lambda i, e, c: (mod_base + (i // tiles_per_seq) * N_MOD + k, 0, 0))

    return pl.pallas_call(
        kern,
        grid=(n // tm, N_EXPERTS, n_chunks),
        in_specs=[
            pl.BlockSpec((tm, D_MODEL), lambda i, e, c: (i, 0)),
            _resident((1, D_MODEL), lambda i, e, c: (0, 0)),
            modspec(3), modspec(4),
            _resident((D_MODEL, LANES), lambda i, e, c: (0, 0)),
            pl.BlockSpec((1, D_MODEL, FF_CHUNK), lambda i, e, c: (e, 0, c)),
            pl.BlockSpec((1, D_MODEL, FF_CHUNK), lambda i, e, c: (e, 0, c)),
            pl.BlockSpec((1, FF_CHUNK, D_MODEL), lambda i, e, c: (e, c, 0)),
            _resident((1, D_MODEL), lambda i, e, c: (0, 0)),
            modspec(5),
        ],
        out_specs=pl.BlockSpec((tm, D_MODEL), lambda i, e, c: (i, 0)),
        out_shape=jax.ShapeDtypeStruct((n, D_MODEL), F32),
        scratch_shapes=[
            pltpu.VMEM((tm, D_MODEL), BF16),
            pltpu.VMEM((tm, LANES), F32),
            pltpu.VMEM((tm, D_MODEL), F32),
        ],
        compiler_params=_params(3, n_arbitrary=2),
        name="l1_moe",
    )(x2d, g_pre, mod_r, mod_r, rw, wg, wu, wd, g_post, mod_r)


def kernel(x, c, mod_w, mod_b, norm_g, l0_w_in, l0_w_out, l0_ffn_w_gate, l0_ffn_w_up, l0_ffn_w_down,
           l1_w_in, l1_q_norm_g, l1_k_norm_g, l1_w_out, l1_router_w, l1_exp_w_gate, l1_exp_w_up,
           l1_exp_w_down):
    batch, seq, d = x.shape
    assert d == D_MODEL and seq % 2048 == 0
    n = batch * seq
    x2d = x.reshape(n, d)
    mod = _mod_vectors(c, mod_w, mod_b)
    mod_r = mod.reshape(mod.shape[0] * batch * N_MOD, 1, d)
    gains = norm_g.reshape(norm_g.shape[0], norm_g.shape[1], 1, d)

    base0 = 0
    gw = 3 * d
    o_list, lse_list = [], []
    for gi, (_, dil) in enumerate(DILATED_GROUPS):
        w_g = l0_w_in[:, gi * gw:(gi + 1) * gw].astype(BF16)
        qkv = _proj0(x2d, mod_r, base0, gains[0, 0], w_g, dil, batch, seq)
        o_g, lse_g = _attn0(qkv, dil)
        o_list.append(o_g)
        lse_list.append(lse_g)
    x2d = _merge_out(o_list, lse_list, x2d, mod_r, base0, l0_w_out.astype(BF16), gains[0, 1], batch, seq)
    x2d = _ffn(x2d, mod_r, base0, gains[0, 2], gains[0, 3], l0_ffn_w_gate.astype(BF16),
               l0_ffn_w_up.astype(BF16), l0_ffn_w_down.astype(BF16), seq)

    base1 = batch * N_MOD
    q, kd, vd = _proj1(x2d, mod_r, base1, gains[1, 0], l1_w_in.astype(BF16), l1_q_norm_g, l1_k_norm_g, seq)
    o = _attn1(q, kd, vd, batch, seq)
    x2d = _out_proj(o, x2d, mod_r, base1, l1_w_out.astype(BF16), gains[1, 1], seq)
    x2d = _moe(x2d, mod_r, base1, gains[1, 2], gains[1, 3], l1_router_w, l1_exp_w_gate.astype(BF16),
               l1_exp_w_up.astype(BF16), l1_exp_w_down.astype(BF16), seq)
    return x2d.reshape(batch, seq, d)
```

```python
import functools
import math

import numpy as np
import jax
import jax.numpy as jnp
from jax import lax
from jax.experimental import pallas as pl
from jax.experimental.pallas import tpu as pltpu

F32 = jnp.float32
BF16 = jnp.bfloat16

D_MODEL = 1024
N_HEADS = 16
HEAD_DIM = 64
LANES = 128
N_PAIRS = D_MODEL // LANES
DILATED_GROUPS = ((128, 1), (512, 4), (2048, 16))
N_SIDE = 64
N_KV_HEADS = 4
Q_PER_KV = N_HEADS // N_KV_HEADS
GRID_W = 64
ROPE_THETA = 10000.0
D_FF = 3584
N_EXPERTS = 8
N_MOD = 6
EPS = 1e-6
NEG_BIG = -1e30
VMEM_LIMIT_BYTES = 56 * 1024 * 1024


def _params(n_axes, n_arbitrary=0):
    sem = ("parallel",) * (n_axes - n_arbitrary) + ("arbitrary",) * n_arbitrary
    return pltpu.CompilerParams(dimension_semantics=sem, vmem_limit_bytes=VMEM_LIMIT_BYTES)


def _resident(shape, index_map):
    return pl.BlockSpec(shape, index_map, pipeline_mode=pl.Buffered(1))


def _rms(x, g):
    ms = jnp.mean(x * x, axis=-1, keepdims=True)
    return x * lax.rsqrt(ms + EPS) * g


def _prenorm(x, g, shift, scale):
    return _rms(x, g) * (1.0 + scale) + shift


def _sigmoid(x):
    return 1.0 / (1.0 + jnp.exp(-x))


def _mod_kernel(c_ref, w_ref, b_ref, o_ref):
    c = c_ref[...]
    ca = c * _sigmoid(c)
    o_ref[0] = jnp.dot(ca, w_ref[0], preferred_element_type=F32, precision=lax.Precision.HIGHEST) + b_ref[0]


def _mod_vectors(c, mod_w, mod_b):
    depth, d, n6 = mod_w.shape
    b = c.shape[0]
    tn = 1536
    return pl.pallas_call(
        _mod_kernel,
        grid=(depth, n6 // tn),
        in_specs=[
            pl.BlockSpec((b, d), lambda l, j: (0, 0)),
            pl.BlockSpec((1, d, tn), lambda l, j: (l, 0, j)),
            pl.BlockSpec((1, 1, tn), lambda l, j: (l, 0, j)),
        ],
        out_specs=pl.BlockSpec((1, b, tn), lambda l, j: (l, 0, j)),
        out_shape=jax.ShapeDtypeStruct((depth, b, n6), F32),
        compiler_params=_params(2),
        name="mod_vectors",
    )(c, mod_w, mod_b.reshape(depth, 1, n6))


PERM_SUB = 256


def _perm_matrix(d):
    p = np.zeros((PERM_SUB, PERM_SUB), np.float32)
    j = np.arange(PERM_SUB)
    p[(j % d) * (PERM_SUB // d) + j // d, j] = 1.0
    return p


def _proj0_kernel(x_ref, g_ref, sh_ref, sc_ref, p_ref, w_ref, o_ref, h_scr, *, d, tm, cw):
    h = _prenorm(x_ref[...], g_ref[...], sh_ref[...], sc_ref[...]).astype(BF16)
    per = tm // d
    if d == 1:
        h_scr[...] = h
    else:
        sub_per = PERM_SUB // d
        for s in range(tm // PERM_SUB):
            hs = jnp.dot(p_ref[...], h[s * PERM_SUB:(s + 1) * PERM_SUB], preferred_element_type=F32).astype(BF16)
            for r in range(d):
                h_scr[r * per + s * sub_per:r * per + (s + 1) * sub_per, :] = hs[r * sub_per:(r + 1) * sub_per]
    hp = h_scr[...]
    for j in range(w_ref.shape[1] // cw):
        res = jnp.dot(hp, w_ref[:, j * cw:(j + 1) * cw], preferred_element_type=F32).astype(BF16)
        for r in range(d):
            o_ref[r, :, j * cw:(j + 1) * cw] = res[r * per:(r + 1) * per]


def _proj0(x2d, mod_r, mod_base, g, w, d, batch, seq):
    n = x2d.shape[0]
    tm = 512
    tiles_per_seq = seq // tm
    ncol = w.shape[1]
    kern = functools.partial(_proj0_kernel, d=d, tm=tm, cw=512)
    perm = jnp.asarray(_perm_matrix(d), BF16)
    return pl.pallas_call(
        kern,
        grid=(n // tm,),
        in_specs=[
            pl.BlockSpec((tm, D_MODEL), lambda i: (i, 0)),
            _resident((1, D_MODEL), lambda i: (0, 0)),
            pl.BlockSpec((None, 1, D_MODEL), lambda i: (mod_base + (i // tiles_per_seq) * N_MOD + 0, 0, 0)),
            pl.BlockSpec((None, 1, D_MODEL), lambda i: (mod_base + (i // tiles_per_seq) * N_MOD + 1, 0, 0)),
            _resident((PERM_SUB, PERM_SUB), lambda i: (0, 0)),
            _resident((D_MODEL, ncol), lambda i: (0, 0)),
        ],
        out_specs=pl.BlockSpec((None, d, tm // d, ncol), lambda i: (i // tiles_per_seq, 0, i % tiles_per_seq, 0)),
        out_shape=jax.ShapeDtypeStruct((batch, d, seq // d, ncol), BF16),
        scratch_shapes=[pltpu.VMEM((tm, D_MODEL), BF16)],
        compiler_params=_params(1),
        name=f"l0_qkv_d{d}",
    )(x2d, g, mod_r, mod_r, perm, w)


ATT0_SQ = 128
ATT0_TK = ATT0_SQ + 2 * N_SIDE


def _attn0_kernel(q_ref, kp_ref, kc_ref, kn_ref, vp_ref, vc_ref, vn_ref, o_ref, lse_ref, kbuf, vbuf,
                  *, d, tq, length):
    i = pl.program_id(2)
    kbuf[0:N_SIDE] = kp_ref[...]
    kbuf[N_SIDE:N_SIDE + tq] = kc_ref[...]
    kbuf[N_SIDE + tq:] = kn_ref[...]
    vbuf[0:N_SIDE] = vp_ref[...]
    vbuf[N_SIDE:N_SIDE + tq] = vc_ref[...]
    vbuf[N_SIDE + tq:] = vn_ref[...]

    lane = lax.broadcasted_iota(jnp.int32, (ATT0_SQ, LANES), 1)
    low = lane < HEAD_DIM
    a_idx = lax.broadcasted_iota(jnp.int32, (ATT0_SQ, ATT0_TK), 0)
    c_idx = lax.broadcasted_iota(jnp.int32, (ATT0_SQ, ATT0_TK), 1)
    absrel = jnp.abs(c_idx - N_SIDE - a_idx)
    for j in range(tq // ATT0_SQ):
        u_key = i * tq + (j * ATT0_SQ - N_SIDE) + c_idx
        valid = (absrel <= N_SIDE) & (u_key >= 0) & (u_key < length)
        base = jnp.where(valid, (-float(d)) * absrel.astype(F32), NEG_BIG)
        lse_tile = jnp.zeros((ATT0_SQ, LANES), F32)
        for p in range(N_PAIRS):
            cols = slice(p * LANES, (p + 1) * LANES)
            q2 = q_ref[j * ATT0_SQ:(j + 1) * ATT0_SQ, cols]
            zero = jnp.zeros_like(q2)
            qs = jnp.concatenate([jnp.where(low, q2, zero), jnp.where(low, zero, q2)], axis=0)
            k2 = kbuf[j * ATT0_SQ:j * ATT0_SQ + ATT0_TK, cols]
            v2 = vbuf[j * ATT0_SQ:j * ATT0_SQ + ATT0_TK, cols]
            s = lax.dot_general(qs, k2, (((1,), (1,)), ((), ())), preferred_element_type=F32)
            s = s * (HEAD_DIM ** -0.5)
            ps, ls = [], []
            for hh in range(2):
                head = 2 * p + hh
                slope = 2.0 ** (-8.0 * (head + 1) / N_HEADS)
                sh = s[hh * ATT0_SQ:(hh + 1) * ATT0_SQ] + slope * base
                m = jnp.max(sh, axis=-1, keepdims=True)
                e = jnp.exp(sh - m)
                l = jnp.sum(e, axis=-1, keepdims=True)
                ps.append(e.astype(BF16))
                ls.append(l)
                lse_tile = jnp.where(lane == head, m + jnp.log(l), lse_tile)
            o = jnp.dot(jnp.concatenate(ps, axis=0), v2, preferred_element_type=F32)
            o2 = jnp.where(low, o[:ATT0_SQ] / ls[0], o[ATT0_SQ:] / ls[1])
            o_ref[j * ATT0_SQ:(j + 1) * ATT0_SQ, cols] = o2.astype(BF16)
        lse_ref[j * ATT0_SQ:(j + 1) * ATT0_SQ, :] = lse_tile


def _attn0(qkv, d):
    batch, _, length, _ = qkv.shape
    tq = min(512, length)
    nb = tq // N_SIDE
    last = length // N_SIDE - 1
    kern = functools.partial(_attn0_kernel, d=d, tq=tq, length=length)

    def main(col):
        return pl.BlockSpec((None, None, tq, D_MODEL), lambda b, r, i: (b, r, i, col))

    def prev(col):
        return pl.BlockSpec((None, None, N_SIDE, D_MODEL), lambda b, r, i: (b, r, jnp.maximum(i * nb - 1, 0), col))

    def nxt(col):
        return pl.BlockSpec((None, None, N_SIDE, D_MODEL), lambda b, r, i: (b, r, jnp.minimum((i + 1) * nb, last), col))

    return pl.pallas_call(
        kern,
        grid=(batch, d, length // tq),
        in_specs=[main(0), prev(1), main(1), nxt(1), prev(2), main(2), nxt(2)],
        out_specs=[
            pl.BlockSpec((None, None, tq, D_MODEL), lambda b, r, i: (b, r, i, 0)),
            pl.BlockSpec((None, None, tq, LANES), lambda b, r, i: (b, r, i, 0)),
        ],
        out_shape=[
            jax.ShapeDtypeStruct((batch, d, length, D_MODEL), BF16),
            jax.ShapeDtypeStruct((batch, d, length, LANES), F32),
        ],
        scratch_shapes=[
            pltpu.VMEM((tq + 2 * N_SIDE, D_MODEL), BF16),
            pltpu.VMEM((tq + 2 * N_SIDE, D_MODEL), BF16),
        ],
        compiler_params=_params(3),
        name=f"l0_attn_d{d}",
    )(qkv, qkv, qkv, qkv, qkv, qkv, qkv)


def _split3(x):
    hi = x.astype(BF16)
    r1 = x - hi.astype(F32)
    mid = r1.astype(BF16)
    lo = (r1 - mid.astype(F32)).astype(BF16)
    return hi, mid, lo


def _unpermute(pt, blk_ref, d, tm, exact_f32):
    sub_per = PERM_SUB // d
    outs = []
    for s in range(tm // PERM_SUB):
        src = jnp.concatenate([blk_ref[r, s * sub_per:(s + 1) * sub_per, :] for r in range(d)], axis=0)
        if exact_f32:
            parts = _split3(src)
            nat = sum(jnp.dot(pt, part, preferred_element_type=F32) for part in parts)
        else:
            nat = jnp.dot(pt, src, preferred_element_type=F32)
        outs.append(nat)
    return jnp.concatenate(outs, axis=0)


def _merge_out_kernel(o0_ref, o1_ref, o2_ref, l0_ref, l1_ref, l2_ref, p1_ref, p2_ref, e_ref,
                      x_ref, w_ref, g_ref, gate_ref, out_ref, *, tm):
    dils = [dil for _, dil in DILATED_GROUPS]
    o_refs = (o0_ref, o1_ref, o2_ref)
    l_refs = (l0_ref, l1_ref, l2_ref)
    pts = (None, p1_ref[...], p2_ref[...])
    os_, ls_ = [], []
    for gi, d in enumerate(dils):
        if d == 1:
            os_.append(o_refs[gi][0].astype(F32))
            ls_.append(l_refs[gi][0])
        else:
            os_.append(_unpermute(pts[gi], o_refs[gi], d, tm, False))
            ls_.append(_unpermute(pts[gi], l_refs[gi], d, tm, True))
    m = jnp.maximum(jnp.maximum(ls_[0], ls_[1]), ls_[2])
    es = [jnp.exp(l - m) for l in ls_]
    den = es[0] + es[1] + es[2]
    acc = jnp.zeros((tm, D_MODEL), F32)
    for gi in range(3):
        wgt = es[gi] / den
        hi, mid, lo = _split3(wgt)
        wfull = (jnp.dot(hi, e_ref[...], preferred_element_type=F32)
                 + jnp.dot(mid, e_ref[...], preferred_element_type=F32)
                 + jnp.dot(lo, e_ref[...], preferred_element_type=F32))
        acc = acc + wfull * os_[gi]
    y = jnp.dot(acc.astype(BF16), w_ref[...], preferred_element_type=F32)
    out_ref[...] = x_ref[...] + gate_ref[...] * _rms(y, g_ref[...])


def _merge_out(o_list, lse_list, x2d, mod_r, mod_base, w_out, g, batch, seq):
    n = x2d.shape[0]
    tm = 512
    tiles_per_seq = seq // tm
    kern = functools.partial(_merge_out_kernel, tm=tm)
    dils = [dil for _, dil in DILATED_GROUPS]
    expand = np.zeros((LANES, D_MODEL), np.float32)
    for h in range(N_HEADS):
        expand[h, h * HEAD_DIM:(h + 1) * HEAD_DIM] = 1.0

    def grouped(d, w):
        return pl.BlockSpec((None, d, tm // d, w), lambda i: (i // tiles_per_seq, 0, i % tiles_per_seq, 0))

    in_specs = ([grouped(d, D_MODEL) for d in dils] + [grouped(d, LANES) for d in dils] + [
        _resident((PERM_SUB, PERM_SUB), lambda i: (0, 0)),
        _resident((PERM_SUB, PERM_SUB), lambda i: (0, 0)),
        _resident((LANES, D_MODEL), lambda i: (0, 0)),
        pl.BlockSpec((tm, D_MODEL), lambda i: (i, 0)),
        _resident((D_MODEL, D_MODEL), lambda i: (0, 0)),
        _resident((1, D_MODEL), lambda i: (0, 0)),
        pl.BlockSpec((None, 1, D_MODEL), lambda i: (mod_base + (i // tiles_per_seq) * N_MOD + 2, 0, 0)),
    ])
    return pl.pallas_call(
        kern,
        grid=(n // tm,),
        in_specs=in_specs,
        out_specs=pl.BlockSpec((tm, D_MODEL), lambda i: (i, 0)),
        out_shape=jax.ShapeDtypeStruct((n, D_MODEL), F32),
        compiler_params=_params(1),
        name="l0_merge_out",
    )(*o_list, *lse_list,
      jnp.asarray(_perm_matrix(dils[1]).T, BF16), jnp.asarray(_perm_matrix(dils[2]).T, BF16),
      jnp.asarray(expand, BF16), x2d, w_out, g, mod_r)


def _out_kernel(o_ref, x_ref, w_ref, g_ref, gate_ref, out_ref):
    y = jnp.dot(o_ref[...], w_ref[...], preferred_element_type=F32)
    out_ref[...] = x_ref[...] + gate_ref[...] * _rms(y, g_ref[...])


def _out_proj(o2d, x2d, mod_r, mod_base, w_out, g, seq):
    n = x2d.shape[0]
    tm = 512
    tiles_per_seq = seq // tm
    return pl.pallas_call(
        _out_kernel,
        grid=(n // tm,),
        in_specs=[
            pl.BlockSpec((tm, D_MODEL), lambda i: (i, 0)),
            pl.BlockSpec((tm, D_MODEL), lambda i: (i, 0)),
            _resident((D_MODEL, D_MODEL), lambda i: (0, 0)),
            _resident((1, D_MODEL), lambda i: (0, 0)),
            pl.BlockSpec((None, 1, D_MODEL), lambda i: (mod_base + (i // tiles_per_seq) * N_MOD + 2, 0, 0)),
        ],
        out_specs=pl.BlockSpec((tm, D_MODEL), lambda i: (i, 0)),
        out_shape=jax.ShapeDtypeStruct((n, D_MODEL), F32),
        compiler_params=_params(1),
        name="l1_out_proj",
    )(o2d, x2d, w_out, g, mod_r)


FF_CHUNK = 512


def _swiglu_acc(h, wg_ref, wu_ref, wd_ref, acc):
    for c in range(wg_ref.shape[-1] // FF_CHUNK):
        cols = slice(c * FF_CHUNK, (c + 1) * FF_CHUNK)
        a = jnp.dot(h, wg_ref[:, cols], preferred_element_type=F32)
        u = jnp.dot(h, wu_ref[:, cols], preferred_element_type=F32)
        t = (a * _sigmoid(a) * u).astype(BF16)
        acc = acc + jnp.dot(t, wd_ref[cols, :], preferred_element_type=F32)
    return acc


def _ffn_kernel(x_ref, g_ref, sh_ref, sc_ref, wg_ref, wu_ref, wd_ref, g2_ref, gate_ref, out_ref):
    x = x_ref[...]
    h = _prenorm(x, g_ref[...], sh_ref[...], sc_ref[...]).astype(BF16)
    y = _swiglu_acc(h, wg_ref, wu_ref, wd_ref, jnp.zeros(x.shape, F32))
    out_ref[...] = x + gate_ref[...] * _rms(y, g2_ref[...])


def _ffn(x2d, mod_r, mod_base, g_pre, g_post, wg, wu, wd, seq):
    n = x2d.shape[0]
    tm = 512
    tiles_per_seq = seq // tm

    def modspec(k):
        return pl.BlockSpec((None, 1, D_MODEL), lambda i: (mod_base + (i // tiles_per_seq) * N_MOD + k, 0, 0))

    return pl.pallas_call(
        _ffn_kernel,
        grid=(n // tm,),
        in_specs=[
            pl.BlockSpec((tm, D_MODEL), lambda i: (i, 0)),
            _resident((1, D_MODEL), lambda i: (0, 0)),
            modspec(3), modspec(4),
            _resident((D_MODEL, D_FF), lambda i: (0, 0)),
            _resident((D_MODEL, D_FF), lambda i: (0, 0)),
            _resident((D_FF, D_MODEL), lambda i: (0, 0)),
            _resident((1, D_MODEL), lambda i: (0, 0)),
            modspec(5),
        ],
        out_specs=pl.BlockSpec((tm, D_MODEL), lambda i: (i, 0)),
        out_shape=jax.ShapeDtypeStruct((n, D_MODEL), F32),
        compiler_params=_params(1),
        name="l0_ffn",
    )(x2d, g_pre, mod_r, mod_r, wg, wu, wd, g_post, mod_r)


ATT1_TK = 512
VT_ROWS = HEAD_DIM + 16


def _proj1_kernel(x_ref, g_ref, sh_ref, sc_ref, w_ref, m2_ref, qg_ref, kg_ref, cos_ref, sn_ref, sp_ref,
                  qt_ref, kd_ref, vt_ref):
    h = _prenorm(x_ref[...], g_ref[...], sh_ref[...], sc_ref[...]).astype(BF16)
    proj = jnp.dot(h, w_ref[...], preferred_element_type=F32)
    tm = proj.shape[0]
    lane = lax.broadcasted_iota(jnp.int32, (tm, LANES), 1)
    low = lane < HEAD_DIM
    cos, sn, sp = cos_ref[...], sn_ref[...], sp_ref[...]
    n_q = D_MODEL // LANES
    n_k = N_KV_HEADS * HEAD_DIM // LANES
    for c in range(n_q + n_k):
        z = proj[:, c * LANES:(c + 1) * LANES]
        zz = z * z
        hi = zz.astype(BF16)
        lo = (zz - hi.astype(F32)).astype(BF16)
        ms = jnp.dot(hi, m2_ref[...], preferred_element_type=F32) + jnp.dot(lo, m2_ref[...], preferred_element_type=F32)
        gain = qg_ref[...] if c < n_q else kg_ref[...]
        zn = z * lax.rsqrt(ms + EPS) * gain
        zr = zn * cos + pltpu.roll(zn, LANES - 16, 1) * sn + pltpu.roll(zn, 16, 1) * sp
        if c < n_q:
            qt_ref[c] = (zr * (HEAD_DIM ** -0.5)).T.astype(BF16)
        else:
            j = c - n_q
            sw = pltpu.roll(zr, HEAD_DIM, 1)
            kd_ref[:, (2 * j) * LANES:(2 * j + 1) * LANES] = jnp.where(low, zr, sw).astype(BF16)
            kd_ref[:, (2 * j + 1) * LANES:(2 * j + 2) * LANES] = jnp.where(low, sw, zr).astype(BF16)
    v_base = D_MODEL + N_KV_HEADS * HEAD_DIM
    ones = jnp.ones((VT_ROWS - HEAD_DIM, tm), BF16)
    for j in range(n_k):
        zt = proj[:, v_base + j * LANES:v_base + (j + 1) * LANES].T
        for hh in range(2):
            vt_ref[2 * j + hh, 0, 0:HEAD_DIM, :] = zt[hh * HEAD_DIM:(hh + 1) * HEAD_DIM].astype(BF16)
            vt_ref[2 * j + hh, 0, HEAD_DIM:, :] = ones


def _rope_tables(seq):
    lane = np.arange(LANES)
    dd = lane % HEAD_DIM
    blk = dd // (HEAD_DIM // 2)
    idx = dd % (HEAD_DIM // 2)
    half = HEAD_DIM // 4
    fi = idx % half
    first = idx < half
    freqs = jnp.asarray(ROPE_THETA, F32) ** (-jnp.arange(half, dtype=F32) / half)
    t = jnp.arange(seq)
    row = (t // GRID_W).astype(F32)
    col = (t % GRID_W).astype(F32)
    pos = jnp.where(jnp.asarray(blk == 0)[None, :], row[:, None], col[:, None])
    ang = pos * freqs[jnp.asarray(fi)][None, :]
    cos = jnp.cos(ang)
    sin = jnp.sin(ang)
    first = jnp.asarray(first)[None, :]
    return cos, jnp.where(first, -sin, 0.0), jnp.where(first, 0.0, sin)


def _proj1(x2d, mod_r, mod_base, g, w, qg, kg, batch, seq):
    n = x2d.shape[0]
    tm = ATT1_TK
    tiles_per_seq = seq // tm
    ncol = w.shape[1]
    kvw = N_KV_HEADS * LANES
    m2 = np.zeros((LANES, LANES), np.float32)
    m2[:HEAD_DIM, :HEAD_DIM] = 1.0 / HEAD_DIM
    m2[HEAD_DIM:, HEAD_DIM:] = 1.0 / HEAD_DIM
    cos, sn, sp = _rope_tables(seq)
    qg2 = jnp.tile(qg.astype(F32), 2).reshape(1, LANES)
    kg2 = jnp.tile(kg.astype(F32), 2).reshape(1, LANES)

    def modspec(k):
        return pl.BlockSpec((None, 1, D_MODEL), lambda i: (mod_base + (i // tiles_per_seq) * N_MOD + k, 0, 0))

    def table():
        return pl.BlockSpec((tm, LANES), lambda i: (i % tiles_per_seq, 0))

    return pl.pallas_call(
        _proj1_kernel,
        grid=(n // tm,),
        in_specs=[
            pl.BlockSpec((tm, D_MODEL), lambda i: (i, 0)),
            _resident((1, D_MODEL), lambda i: (0, 0)),
            modspec(0), modspec(1),
            _resident((D_MODEL, ncol), lambda i: (0, 0)),
            _resident((LANES, LANES), lambda i: (0, 0)),
            _resident((1, LANES), lambda i: (0, 0)),
            _resident((1, LANES), lambda i: (0, 0)),
            table(), table(), table(),
        ],
        out_specs=[
            pl.BlockSpec((None, N_PAIRS, LANES, tm), lambda i: (i // tiles_per_seq, 0, 0, i % tiles_per_seq)),
            pl.BlockSpec((tm, kvw), lambda i: (i, 0)),
            pl.BlockSpec((None, N_KV_HEADS, 1, VT_ROWS, tm),
                         lambda i: (i // tiles_per_seq, 0, i % tiles_per_seq, 0, 0)),
        ],
        out_shape=[
            jax.ShapeDtypeStruct((batch, N_PAIRS, LANES, seq), BF16),
            jax.ShapeDtypeStruct((n, kvw), BF16),
            jax.ShapeDtypeStruct((batch, N_KV_HEADS, tiles_per_seq, VT_ROWS, tm), BF16),
        ],
        compiler_params=_params(1),
        name="l1_qkv_rope",
    )(x2d, g, mod_r, mod_r, w, jnp.asarray(m2, BF16), qg2, kg2, cos, sn, sp)


def _attn1_kernel(qt_ref, k_ref, vt_ref, o_ref, w_scr, st_scr, m_scr, acc_scr, *, tq, tk, seq):
    row = lax.broadcasted_iota(jnp.int32, (LANES, tq), 0)
    top = row < HEAD_DIM
    for h in range(Q_PER_KV):
        blk = qt_ref[h // 2]
        keep = top if h % 2 == 0 else jnp.logical_not(top)
        w_scr[h] = jnp.where(keep, blk, jnp.zeros_like(blk))
    m_scr[...] = jnp.full(m_scr.shape, NEG_BIG, F32)
    acc_scr[...] = jnp.zeros(acc_scr.shape, F32)
    nk = seq // tk

    def scores(h, kt, buf):
        start = pl.multiple_of(kt * tk, tk)
        st_scr[buf, h] = jnp.dot(k_ref[pl.ds(start, tk), :], w_scr[h], preferred_element_type=F32)

    def softmax_pv(h, kt, buf):
        st = st_scr[buf, h]
        m_prev = m_scr[h]
        m_new = jnp.maximum(m_prev, jnp.max(st, axis=0, keepdims=True))
        p = jnp.exp(st - m_new).astype(BF16)
        acc_scr[h] = jnp.exp(m_prev - m_new) * acc_scr[h] + jnp.dot(vt_ref[kt], p, preferred_element_type=F32)
        m_scr[h] = m_new

    def step(kt, cur, nxt):
        scores(0, kt + 1, nxt)
        scores(1, kt + 1, nxt)
        softmax_pv(0, kt, cur)
        scores(2, kt + 1, nxt)
        softmax_pv(1, kt, cur)
        scores(3, kt + 1, nxt)
        softmax_pv(2, kt, cur)
        softmax_pv(3, kt, cur)

    for h in range(Q_PER_KV):
        scores(h, 0, 0)

    def body(i, carry):
        step(2 * i, 0, 1)
        step(2 * i + 1, 1, 0)
        return carry

    lax.fori_loop(0, nk // 2 - 1, body, 0)
    step(nk - 2, 0, 1)
    for h in range(Q_PER_KV):
        softmax_pv(h, nk - 1, 1)
    for g2 in range(Q_PER_KV // 2):
        parts = []
        for hh in range(2):
            a = acc_scr[2 * g2 + hh]
            parts.append(a[:HEAD_DIM] / a[HEAD_DIM:HEAD_DIM + 1])
        o_ref[:, g2 * LANES:(g2 + 1) * LANES] = jnp.concatenate(parts, axis=0).T.astype(BF16)


def _attn1(qt, kd, vt, batch, seq):
    tq, tk = 512, ATT1_TK
    qtiles = seq // tq
    qw = Q_PER_KV * HEAD_DIM
    kern = functools.partial(_attn1_kernel, tq=tq, tk=tk, seq=seq)
    return pl.pallas_call(
        kern,
        grid=(batch, N_KV_HEADS, qtiles),
        in_specs=[
            pl.BlockSpec((None, Q_PER_KV // 2, LANES, tq), lambda b, j, i: (b, j, 0, i)),
            pl.BlockSpec((seq, LANES), lambda b, j, i: (b, j)),
            pl.BlockSpec((None, None, seq // tk, VT_ROWS, tk), lambda b, j, i: (b, j, 0, 0, 0)),
        ],
        out_specs=pl.BlockSpec((tq, qw), lambda b, j, i: (b * qtiles + i, j)),
        out_shape=jax.ShapeDtypeStruct((batch * seq, D_MODEL), BF16),
        scratch_shapes=[
            pltpu.VMEM((Q_PER_KV, LANES, tq), BF16),
            pltpu.VMEM((2, Q_PER_KV, tk, tq), F32),
            pltpu.VMEM((Q_PER_KV, 1, tq), F32),
            pltpu.VMEM((Q_PER_KV, VT_ROWS, tq), F32),
        ],
        compiler_params=_params(3),
        name="l1_attn",
    )(qt, kd, vt)


def _moe_kernel(x_ref, g_ref, sh_ref, sc_ref, rw_ref, wg_ref, wu_ref, wd_ref, g2_ref, gate_ref, out_ref,
                h_scr, gates_scr, acc_scr, *, n_chunks):
    e = pl.program_id(1)
    c = pl.program_id(2)
    tm = x_ref.shape[0]
    lane = lax.broadcasted_iota(jnp.int32, (tm, LANES), 1)

    @pl.when((e == 0) & (c == 0))
    def _route():
        h = _prenorm(x_ref[...], g_ref[...], sh_ref[...], sc_ref[...])
        h_scr[...] = h.astype(BF16)
        logits = jnp.dot(h, rw_ref[...], preferred_element_type=F32, precision=lax.Precision.HIGHEST)
        logits = jnp.where(lane < N_EXPERTS, logits, NEG_BIG)
        m1 = jnp.max(logits, axis=-1, keepdims=True)
        i1 = jnp.min(jnp.where(logits == m1, lane, LANES), axis=-1, keepdims=True)
        rest = jnp.where(lane == i1, NEG_BIG, logits)
        m2 = jnp.max(rest, axis=-1, keepdims=True)
        i2 = jnp.min(jnp.where(rest == m2, lane, LANES), axis=-1, keepdims=True)
        t = jnp.exp(m2 - m1)
        w1 = 1.0 / (1.0 + t)
        w2 = t / (1.0 + t)
        gates_scr[...] = jnp.where(lane == i1, w1, 0.0) + jnp.where(lane == i2, w2, 0.0)
        acc_scr[...] = jnp.zeros(acc_scr.shape, F32)

    h = h_scr[...]
    a = jnp.dot(h, wg_ref[0], preferred_element_type=F32)
    u = jnp.dot(h, wu_ref[0], preferred_element_type=F32)
    t = (a * _sigmoid(a) * u).astype(BF16)
    gate_col = jnp.sum(jnp.where(lane == e, gates_scr[...], 0.0), axis=-1, keepdims=True)
    acc_scr[...] += gate_col * jnp.dot(t, wd_ref[0], preferred_element_type=F32)

    @pl.when((e == N_EXPERTS - 1) & (c == n_chunks - 1))
    def _finish():
        out_ref[...] = x_ref[...] + gate_ref[...] * _rms(acc_scr[...], g2_ref[...])


def _moe(x2d, mod_r, mod_base, g_pre, g_post, router_w, wg, wu, wd, seq):
    n = x2d.shape[0]
    tm = 1024
    tiles_per_seq = seq // tm
    n_chunks = D_FF // FF_CHUNK
    rw = jnp.zeros((D_MODEL, LANES), F32).at[:, :N_EXPERTS].set(router_w.astype(F32))
    kern = functools.partial(_moe_kernel, n_chunks=n_chunks)

    def modspec(k):
        return pl.BlockSpec((None, 1, D_MODEL), lambda i, e, c: (mod_base + (i // tiles_per_seq) * N_MOD + k, 0, 0))

    return pl.pallas_call(
        kern,
        grid=(n // tm, N_EXPERTS, n_chunks),
        in_specs=[
            pl.BlockSpec((tm, D_MODEL), lambda i, e, c: (i, 0)),
            _resident((1, D_MODEL), lambda i, e, c: (0, 0)),
            modspec(3), modspec(4),
            _resident((D_MODEL, LANES), lambda i, e, c: (0, 0)),
            pl.BlockSpec((1, D_MODEL, FF_CHUNK), lambda i, e, c: (e, 0, c)),
            pl.BlockSpec((1, D_MODEL, FF_CHUNK), lambda i, e, c: (e, 0, c)),
            pl.BlockSpec((1, FF_CHUNK, D_MODEL), lambda i, e, c: (e, c, 0)),
            _resident((1, D_MODEL), lambda i, e, c: (0, 0)),
            modspec(5),
        ],
        out_specs=pl.BlockSpec((tm, D_MODEL), lambda i, e, c: (i, 0)),
        out_shape=jax.ShapeDtypeStruct((n, D_MODEL), F32),
        scratch_shapes=[
            pltpu.VMEM((tm, D_MODEL), BF16),
            pltpu.VMEM((tm, LANES), F32),
            pltpu.VMEM((tm, D_MODEL), F32),
        ],
        compiler_params=_params(3, n_arbitrary=2),
        name="l1_moe",
    )(x2d, g_pre, mod_r, mod_r, rw, wg, wu, wd, g_post, mod_r)


def kernel(x, c, mod_w, mod_b, norm_g, l0_w_in, l0_w_out, l0_ffn_w_gate, l0_ffn_w_up, l0_ffn_w_down,
           l1_w_in, l1_q_norm_g, l1_k_norm_g, l1_w_out, l1_router_w, l1_exp_w_gate, l1_exp_w_up,
           l1_exp_w_down):
    batch, seq, d = x.shape
    assert d == D_MODEL and seq % 2048 == 0
    n = batch * seq
    x2d = x.reshape(n, d)
    mod = _mod_vectors(c, mod_w, mod_b)
    mod_r = mod.reshape(mod.shape[0] * batch * N_MOD, 1, d)
    gains = norm_g.reshape(norm_g.shape[0], norm_g.shape[1], 1, d)

    base0 = 0
    gw = 3 * d
    o_list, lse_list = [], []
    for gi, (_, dil) in enumerate(DILATED_GROUPS):
        w_g = l0_w_in[:, gi * gw:(gi + 1) * gw].astype(BF16)
        qkv = _proj0(x2d, mod_r, base0, gains[0, 0], w_g, dil, batch, seq)
        o_g, lse_g = _attn0(qkv, dil)
        o_list.append(o_g)
        lse_list.append(lse_g)
    x2d = _merge_out(o_list, lse_list, x2d, mod_r, base0, l0_w_out.astype(BF16), gains[0, 1], batch, seq)
    x2d = _ffn(x2d, mod_r, base0, gains[0, 2], gains[0, 3], l0_ffn_w_gate.astype(BF16),
               l0_ffn_w_up.astype(BF16), l0_ffn_w_down.astype(BF16), seq)

    base1 = batch * N_MOD
    qt, kd, vt = _proj1(x2d, mod_r, base1, gains[1, 0], l1_w_in.astype(BF16), l1_q_norm_g, l1_k_norm_g,
                        batch, seq)
    o = _attn1(qt, kd, vt, batch, seq)
    x2d = _out_proj(o, x2d, mod_r, base1, l1_w_out.astype(BF16), gains[1, 1], seq)
    x2d = _moe(x2d, mod_r, base1, gains[1, 2], gains[1, 3], l1_router_w, l1_exp_w_gate.astype(BF16),
               l1_exp_w_up.astype(BF16), l1_exp_w_down.astype(BF16), seq)
    return x2d.reshape(batch, seq, d)
```

```python
import functools
import math

import numpy as np
import jax
import jax.numpy as jnp
from jax import lax
from jax.experimental import pallas as pl
from jax.experimental.pallas import tpu as pltpu

F32 = jnp.float32
BF16 = jnp.bfloat16

D_MODEL = 1024
N_HEADS = 16
HEAD_DIM = 64
LANES = 128
N_PAIRS = D_MODEL // LANES
DILATED_GROUPS = ((128, 1), (512, 4), (2048, 16))
N_SIDE = 64
N_KV_HEADS = 4
Q_PER_KV = N_HEADS // N_KV_HEADS
GRID_W = 64
ROPE_THETA = 10000.0
D_FF = 3584
N_EXPERTS = 8
N_MOD = 6
EPS = 1e-6
NEG_BIG = -1e30
VMEM_LIMIT_BYTES = 56 * 1024 * 1024


def _params(n_axes, n_arbitrary=0):
    sem = ("parallel",) * (n_axes - n_arbitrary) + ("arbitrary",) * n_arbitrary
    return pltpu.CompilerParams(dimension_semantics=sem, vmem_limit_bytes=VMEM_LIMIT_BYTES)


def _resident(shape, index_map):
    return pl.BlockSpec(shape, index_map, pipeline_mode=pl.Buffered(1))


def _rms(x, g):
    ms = jnp.mean(x * x, axis=-1, keepdims=True)
    return x * lax.rsqrt(ms + EPS) * g


def _prenorm(x, g, shift, scale):
    return _rms(x, g) * (1.0 + scale) + shift


def _sigmoid(x):
    return 1.0 / (1.0 + jnp.exp(-x))


def _mod_kernel(c_ref, w_ref, b_ref, o_ref):
    c = c_ref[...]
    ca = c * _sigmoid(c)
    o_ref[0] = jnp.dot(ca, w_ref[0], preferred_element_type=F32, precision=lax.Precision.HIGHEST) + b_ref[0]


def _mod_vectors(c, mod_w, mod_b):
    depth, d, n6 = mod_w.shape
    b = c.shape[0]
    tn = 1536
    return pl.pallas_call(
        _mod_kernel,
        grid=(depth, n6 // tn),
        in_specs=[
            pl.BlockSpec((b, d), lambda l, j: (0, 0)),
            pl.BlockSpec((1, d, tn), lambda l, j: (l, 0, j)),
            pl.BlockSpec((1, 1, tn), lambda l, j: (l, 0, j)),
        ],
        out_specs=pl.BlockSpec((1, b, tn), lambda l, j: (l, 0, j)),
        out_shape=jax.ShapeDtypeStruct((depth, b, n6), F32),
        compiler_params=_params(2),
        name="mod_vectors",
    )(c, mod_w, mod_b.reshape(depth, 1, n6))


PERM_SUB = 256


def _perm_matrix(d):
    p = np.zeros((PERM_SUB, PERM_SUB), np.float32)
    j = np.arange(PERM_SUB)
    p[(j % d) * (PERM_SUB // d) + j // d, j] = 1.0
    return p


def _proj0_kernel(x_ref, g_ref, sh_ref, sc_ref, p_ref, w_ref, o_ref, h_scr, *, d, tm, cw):
    h = _prenorm(x_ref[...], g_ref[...], sh_ref[...], sc_ref[...]).astype(BF16)
    per = tm // d
    if d == 1:
        h_scr[...] = h
    else:
        sub_per = PERM_SUB // d
        for s in range(tm // PERM_SUB):
            hs = jnp.dot(p_ref[...], h[s * PERM_SUB:(s + 1) * PERM_SUB], preferred_element_type=F32).astype(BF16)
            for r in range(d):
                h_scr[r * per + s * sub_per:r * per + (s + 1) * sub_per, :] = hs[r * sub_per:(r + 1) * sub_per]
    hp = h_scr[...]
    for j in range(w_ref.shape[1] // cw):
        res = jnp.dot(hp, w_ref[:, j * cw:(j + 1) * cw], preferred_element_type=F32).astype(BF16)
        for r in range(d):
            o_ref[r, :, j * cw:(j + 1) * cw] = res[r * per:(r + 1) * per]


def _proj0(x2d, mod_r, mod_base, g, w, d, batch, seq):
    n = x2d.shape[0]
    tm = 512
    tiles_per_seq = seq // tm
    ncol = w.shape[1]
    kern = functools.partial(_proj0_kernel, d=d, tm=tm, cw=512)
    perm = jnp.asarray(_perm_matrix(d), BF16)
    return pl.pallas_call(
        kern,
        grid=(n // tm,),
        in_specs=[
            pl.BlockSpec((tm, D_MODEL), lambda i: (i, 0)),
            _resident((1, D_MODEL), lambda i: (0, 0)),
            pl.BlockSpec((None, 1, D_MODEL), lambda i: (mod_base + (i // tiles_per_seq) * N_MOD + 0, 0, 0)),
            pl.BlockSpec((None, 1, D_MODEL), lambda i: (mod_base + (i // tiles_per_seq) * N_MOD + 1, 0, 0)),
            _resident((PERM_SUB, PERM_SUB), lambda i: (0, 0)),
            _resident((D_MODEL, ncol), lambda i: (0, 0)),
        ],
        out_specs=pl.BlockSpec((None, d, tm // d, ncol), lambda i: (i // tiles_per_seq, 0, i % tiles_per_seq, 0)),
        out_shape=jax.ShapeDtypeStruct((batch, d, seq // d, ncol), BF16),
        scratch_shapes=[pltpu.VMEM((tm, D_MODEL), BF16)],
        compiler_params=_params(1),
        name=f"l0_qkv_d{d}",
    )(x2d, g, mod_r, mod_r, perm, w)


ATT0_SQ = 128
ATT0_TK = ATT0_SQ + 2 * N_SIDE


def _attn0_kernel(q_ref, kp_ref, kc_ref, kn_ref, vp_ref, vc_ref, vn_ref, o_ref, lse_ref, kbuf, vbuf,
                  *, d, tq, length):
    i = pl.program_id(2)
    kbuf[0:N_SIDE] = kp_ref[...]
    kbuf[N_SIDE:N_SIDE + tq] = kc_ref[...]
    kbuf[N_SIDE + tq:] = kn_ref[...]
    vbuf[0:N_SIDE] = vp_ref[...]
    vbuf[N_SIDE:N_SIDE + tq] = vc_ref[...]
    vbuf[N_SIDE + tq:] = vn_ref[...]

    lane = lax.broadcasted_iota(jnp.int32, (ATT0_SQ, LANES), 1)
    low = lane < HEAD_DIM
    a_idx = lax.broadcasted_iota(jnp.int32, (ATT0_SQ, ATT0_TK), 0)
    c_idx = lax.broadcasted_iota(jnp.int32, (ATT0_SQ, ATT0_TK), 1)
    absrel = jnp.abs(c_idx - N_SIDE - a_idx)
    for j in range(tq // ATT0_SQ):
        u_key = i * tq + (j * ATT0_SQ - N_SIDE) + c_idx
        valid = (absrel <= N_SIDE) & (u_key >= 0) & (u_key < length)
        base = jnp.where(valid, (-float(d)) * absrel.astype(F32), NEG_BIG)
        lse_tile = jnp.zeros((ATT0_SQ, LANES), F32)
        for p in range(N_PAIRS):
            cols = slice(p * LANES, (p + 1) * LANES)
            q2 = q_ref[j * ATT0_SQ:(j + 1) * ATT0_SQ, cols]
            zero = jnp.zeros_like(q2)
            qs = jnp.concatenate([jnp.where(low, q2, zero), jnp.where(low, zero, q2)], axis=0)
            k2 = kbuf[j * ATT0_SQ:j * ATT0_SQ + ATT0_TK, cols]
            v2 = vbuf[j * ATT0_SQ:j * ATT0_SQ + ATT0_TK, cols]
            s = lax.dot_general(qs, k2, (((1,), (1,)), ((), ())), preferred_element_type=F32)
            s = s * (HEAD_DIM ** -0.5)
            ps, ls = [], []
            for hh in range(2):
                head = 2 * p + hh
                slope = 2.0 ** (-8.0 * (head + 1) / N_HEADS)
                sh = s[hh * ATT0_SQ:(hh + 1) * ATT0_SQ] + slope * base
                m = jnp.max(sh, axis=-1, keepdims=True)
                e = jnp.exp(sh - m)
                l = jnp.sum(e, axis=-1, keepdims=True)
                ps.append(e.astype(BF16))
                ls.append(l)
                lse_tile = jnp.where(lane == head, m + jnp.log(l), lse_tile)
            o = jnp.dot(jnp.concatenate(ps, axis=0), v2, preferred_element_type=F32)
            o2 = jnp.where(low, o[:ATT0_SQ] / ls[0], o[ATT0_SQ:] / ls[1])
            o_ref[j * ATT0_SQ:(j + 1) * ATT0_SQ, cols] = o2.astype(BF16)
        lse_ref[j * ATT0_SQ:(j + 1) * ATT0_SQ, :] = lse_tile


def _attn0(qkv, d):
    batch, _, length, _ = qkv.shape
    tq = min(512, length)
    nb = tq // N_SIDE
    last = length // N_SIDE - 1
    kern = functools.partial(_attn0_kernel, d=d, tq=tq, length=length)

    def main(col):
        return pl.BlockSpec((None, None, tq, D_MODEL), lambda b, r, i: (b, r, i, col))

    def prev(col):
        return pl.BlockSpec((None, None, N_SIDE, D_MODEL), lambda b, r, i: (b, r, jnp.maximum(i * nb - 1, 0), col))

    def nxt(col):
        return pl.BlockSpec((None, None, N_SIDE, D_MODEL), lambda b, r, i: (b, r, jnp.minimum((i + 1) * nb, last), col))

    return pl.pallas_call(
        kern,
        grid=(batch, d, length // tq),
        in_specs=[main(0), prev(1), main(1), nxt(1), prev(2), main(2), nxt(2)],
        out_specs=[
            pl.BlockSpec((None, None, tq, D_MODEL), lambda b, r, i: (b, r, i, 0)),
            pl.BlockSpec((None, None, tq, LANES), lambda b, r, i: (b, r, i, 0)),
        ],
        out_shape=[
            jax.ShapeDtypeStruct((batch, d, length, D_MODEL), BF16),
            jax.ShapeDtypeStruct((batch, d, length, LANES), F32),
        ],
        scratch_shapes=[
            pltpu.VMEM((tq + 2 * N_SIDE, D_MODEL), BF16),
            pltpu.VMEM((tq + 2 * N_SIDE, D_MODEL), BF16),
        ],
        compiler_params=_params(3),
        name=f"l0_attn_d{d}",
    )(qkv, qkv, qkv, qkv, qkv, qkv, qkv)


def _split3(x):
    hi = x.astype(BF16)
    r1 = x - hi.astype(F32)
    mid = r1.astype(BF16)
    lo = (r1 - mid.astype(F32)).astype(BF16)
    return hi, mid, lo


def _unpermute(pt, blk_ref, d, tm, exact_f32):
    sub_per = PERM_SUB // d
    outs = []
    for s in range(tm // PERM_SUB):
        src = jnp.concatenate([blk_ref[r, s * sub_per:(s + 1) * sub_per, :] for r in range(d)], axis=0)
        if exact_f32:
            parts = _split3(src)
            nat = sum(jnp.dot(pt, part, preferred_element_type=F32) for part in parts)
        else:
            nat = jnp.dot(pt, src, preferred_element_type=F32)
        outs.append(nat)
    return jnp.concatenate(outs, axis=0)


def _merge_out_kernel(o0_ref, o1_ref, o2_ref, l0_ref, l1_ref, l2_ref, p1_ref, p2_ref, e_ref,
                      x_ref, w_ref, g_ref, gate_ref, out_ref, *, tm):
    dils = [dil for _, dil in DILATED_GROUPS]
    o_refs = (o0_ref, o1_ref, o2_ref)
    l_refs = (l0_ref, l1_ref, l2_ref)
    pts = (None, p1_ref[...], p2_ref[...])
    os_, ls_ = [], []
    for gi, d in enumerate(dils):
        if d == 1:
            os_.append(o_refs[gi][0].astype(F32))
            ls_.append(l_refs[gi][0])
        else:
            os_.append(_unpermute(pts[gi], o_refs[gi], d, tm, False))
            ls_.append(_unpermute(pts[gi], l_refs[gi], d, tm, True))
    m = jnp.maximum(jnp.maximum(ls_[0], ls_[1]), ls_[2])
    es = [jnp.exp(l - m) for l in ls_]
    den = es[0] + es[1] + es[2]
    acc = jnp.zeros((tm, D_MODEL), F32)
    for gi in range(3):
        wgt = es[gi] / den
        hi, mid, lo = _split3(wgt)
        wfull = (jnp.dot(hi, e_ref[...], preferred_element_type=F32)
                 + jnp.dot(mid, e_ref[...], preferred_element_type=F32)
                 + jnp.dot(lo, e_ref[...], preferred_element_type=F32))
        acc = acc + wfull * os_[gi]
    y = jnp.dot(acc.astype(BF16), w_ref[...], preferred_element_type=F32)
    out_ref[...] = x_ref[...] + gate_ref[...] * _rms(y, g_ref[...])


def _merge_out(o_list, lse_list, x2d, mod_r, mod_base, w_out, g, batch, seq):
    n = x2d.shape[0]
    tm = 512
    tiles_per_seq = seq // tm
    kern = functools.partial(_merge_out_kernel, tm=tm)
    dils = [dil for _, dil in DILATED_GROUPS]
    expand = np.zeros((LANES, D_MODEL), np.float32)
    for h in range(N_HEADS):
        expand[h, h * HEAD_DIM:(h + 1) * HEAD_DIM] = 1.0

    def grouped(d, w):
        return pl.BlockSpec((None, d, tm // d, w), lambda i: (i // tiles_per_seq, 0, i % tiles_per_seq, 0))

    in_specs = ([grouped(d, D_MODEL) for d in dils] + [grouped(d, LANES) for d in dils] + [
        _resident((PERM_SUB, PERM_SUB), lambda i: (0, 0)),
        _resident((PERM_SUB, PERM_SUB), lambda i: (0, 0)),
        _resident((LANES, D_MODEL), lambda i: (0, 0)),
        pl.BlockSpec((tm, D_MODEL), lambda i: (i, 0)),
        _resident((D_MODEL, D_MODEL), lambda i: (0, 0)),
        _resident((1, D_MODEL), lambda i: (0, 0)),
        pl.BlockSpec((None, 1, D_MODEL), lambda i: (mod_base + (i // tiles_per_seq) * N_MOD + 2, 0, 0)),
    ])
    return pl.pallas_call(
        kern,
        grid=(n // tm,),
        in_specs=in_specs,
        out_specs=pl.BlockSpec((tm, D_MODEL), lambda i: (i, 0)),
        out_shape=jax.ShapeDtypeStruct((n, D_MODEL), F32),
        compiler_params=_params(1),
        name="l0_merge_out",
    )(*o_list, *lse_list,
      jnp.asarray(_perm_matrix(dils[1]).T, BF16), jnp.asarray(_perm_matrix(dils[2]).T, BF16),
      jnp.asarray(expand, BF16), x2d, w_out, g, mod_r)


def _out_kernel(o_ref, x_ref, w_ref, g_ref, gate_ref, out_ref):
    y = jnp.dot(o_ref[...], w_ref[...], preferred_element_type=F32)
    out_ref[...] = x_ref[...] + gate_ref[...] * _rms(y, g_ref[...])


def _out_proj(o2d, x2d, mod_r, mod_base, w_out, g, seq):
    n = x2d.shape[0]
    tm = 512
    tiles_per_seq = seq // tm
    return pl.pallas_call(
        _out_kernel,
        grid=(n // tm,),
        in_specs=[
            pl.BlockSpec((tm, D_MODEL), lambda i: (i, 0)),
            pl.BlockSpec((tm, D_MODEL), lambda i: (i, 0)),
            _resident((D_MODEL, D_MODEL), lambda i: (0, 0)),
            _resident((1, D_MODEL), lambda i: (0, 0)),
            pl.BlockSpec((None, 1, D_MODEL), lambda i: (mod_base + (i // tiles_per_seq) * N_MOD + 2, 0, 0)),
        ],
        out_specs=pl.BlockSpec((tm, D_MODEL), lambda i: (i, 0)),
        out_shape=jax.ShapeDtypeStruct((n, D_MODEL), F32),
        compiler_params=_params(1),
        name="l1_out_proj",
    )(o2d, x2d, w_out, g, mod_r)


FF_CHUNK = 512


def _swiglu_acc(h, wg_ref, wu_ref, wd_ref, acc):
    for c in range(wg_ref.shape[-1] // FF_CHUNK):
        cols = slice(c * FF_CHUNK, (c + 1) * FF_CHUNK)
        a = jnp.dot(h, wg_ref[:, cols], preferred_element_type=F32)
        u = jnp.dot(h, wu_ref[:, cols], preferred_element_type=F32)
        t = (a * _sigmoid(a) * u).astype(BF16)
        acc = acc + jnp.dot(t, wd_ref[cols, :], preferred_element_type=F32)
    return acc


def _ffn_kernel(x_ref, g_ref, sh_ref, sc_ref, wg_ref, wu_ref, wd_ref, g2_ref, gate_ref, out_ref):
    x = x_ref[...]
    h = _prenorm(x, g_ref[...], sh_ref[...], sc_ref[...]).astype(BF16)
    y = _swiglu_acc(h, wg_ref, wu_ref, wd_ref, jnp.zeros(x.shape, F32))
    out_ref[...] = x + gate_ref[...] * _rms(y, g2_ref[...])


def _ffn(x2d, mod_r, mod_base, g_pre, g_post, wg, wu, wd, seq):
    n = x2d.shape[0]
    tm = 512
    tiles_per_seq = seq // tm

    def modspec(k):
        return pl.BlockSpec((None, 1, D_MODEL), lambda i: (mod_base + (i // tiles_per_seq) * N_MOD + k, 0, 0))

    return pl.pallas_call(
        _ffn_kernel,
        grid=(n // tm,),
        in_specs=[
            pl.BlockSpec((tm, D_MODEL), lambda i: (i, 0)),
            _resident((1, D_MODEL), lambda i: (0, 0)),
            modspec(3), modspec(4),
            _resident((D_MODEL, D_FF), lambda i: (0, 0)),
            _resident((D_MODEL, D_FF), lambda i: (0, 0)),
            _resident((D_FF, D_MODEL), lambda i: (0, 0)),
            _resident((1, D_MODEL), lambda i: (0, 0)),
            modspec(5),
        ],
        out_specs=pl.BlockSpec((tm, D_MODEL), lambda i: (i, 0)),
        out_shape=jax.ShapeDtypeStruct((n, D_MODEL), F32),
        compiler_params=_params(1),
        name="l0_ffn",
    )(x2d, g_pre, mod_r, mod_r, wg, wu, wd, g_post, mod_r)


ATT1_TK = 512
VT_ROWS = HEAD_DIM + 16


def _proj1_kernel(x_ref, g_ref, sh_ref, sc_ref, w_ref, m2_ref, qg_ref, kg_ref, cos_ref, sn_ref, sp_ref,
                  qt_ref, kd_ref, vt_ref):
    h = _prenorm(x_ref[...], g_ref[...], sh_ref[...], sc_ref[...]).astype(BF16)
    proj = jnp.dot(h, w_ref[...], preferred_element_type=F32)
    tm = proj.shape[0]
    lane = lax.broadcasted_iota(jnp.int32, (tm, LANES), 1)
    low = lane < HEAD_DIM
    cos, sn, sp = cos_ref[...], sn_ref[...], sp_ref[...]
    n_q = D_MODEL // LANES
    n_k = N_KV_HEADS * HEAD_DIM // LANES
    for c in range(n_q + n_k):
        z = proj[:, c * LANES:(c + 1) * LANES]
        zz = z * z
        hi = zz.astype(BF16)
        lo = (zz - hi.astype(F32)).astype(BF16)
        ms = jnp.dot(hi, m2_ref[...], preferred_element_type=F32) + jnp.dot(lo, m2_ref[...], preferred_element_type=F32)
        gain = qg_ref[...] if c < n_q else kg_ref[...]
        zn = z * lax.rsqrt(ms + EPS) * gain
        zr = zn * cos + pltpu.roll(zn, LANES - 16, 1) * sn + pltpu.roll(zn, 16, 1) * sp
        if c < n_q:
            qt_ref[c] = (zr * (HEAD_DIM ** -0.5)).T.astype(BF16)
        else:
            j = c - n_q
            sw = pltpu.roll(zr, HEAD_DIM, 1)
            kd_ref[:, (2 * j) * LANES:(2 * j + 1) * LANES] = jnp.where(low, zr, sw).astype(BF16)
            kd_ref[:, (2 * j + 1) * LANES:(2 * j + 2) * LANES] = jnp.where(low, sw, zr).astype(BF16)
    v_base = D_MODEL + N_KV_HEADS * HEAD_DIM
    ones = jnp.ones((VT_ROWS - HEAD_DIM, tm), BF16)
    for j in range(n_k):
        zt = proj[:, v_base + j * LANES:v_base + (j + 1) * LANES].T
        for hh in range(2):
            vt_ref[2 * j + hh, 0, 0:HEAD_DIM, :] = zt[hh * HEAD_DIM:(hh + 1) * HEAD_DIM].astype(BF16)
            vt_ref[2 * j + hh, 0, HEAD_DIM:, :] = ones


def _rope_tables(seq):
    lane = np.arange(LANES)
    dd = lane % HEAD_DIM
    blk = dd // (HEAD_DIM // 2)
    idx = dd % (HEAD_DIM // 2)
    half = HEAD_DIM // 4
    fi = idx % half
    first = idx < half
    freqs = jnp.asarray(ROPE_THETA, F32) ** (-jnp.arange(half, dtype=F32) / half)
    t = jnp.arange(seq)
    row = (t // GRID_W).astype(F32)
    col = (t % GRID_W).astype(F32)
    pos = jnp.where(jnp.asarray(blk == 0)[None, :], row[:, None], col[:, None])
    ang = pos * freqs[jnp.asarray(fi)][None, :]
    cos = jnp.cos(ang)
    sin = jnp.sin(ang)
    first = jnp.asarray(first)[None, :]
    return cos, jnp.where(first, -sin, 0.0), jnp.where(first, 0.0, sin)


def _proj1(x2d, mod_r, mod_base, g, w, qg, kg, batch, seq):
    n = x2d.shape[0]
    tm = ATT1_TK
    tiles_per_seq = seq // tm
    ncol = w.shape[1]
    kvw = N_KV_HEADS * LANES
    m2 = np.zeros((LANES, LANES), np.float32)
    m2[:HEAD_DIM, :HEAD_DIM] = 1.0 / HEAD_DIM
    m2[HEAD_DIM:, HEAD_DIM:] = 1.0 / HEAD_DIM
    cos, sn, sp = _rope_tables(seq)
    qg2 = jnp.tile(qg.astype(F32), 2).reshape(1, LANES)
    kg2 = jnp.tile(kg.astype(F32), 2).reshape(1, LANES)

    def modspec(k):
        return pl.BlockSpec((None, 1, D_MODEL), lambda i: (mod_base + (i // tiles_per_seq) * N_MOD + k, 0, 0))

    def table():
        return pl.BlockSpec((tm, LANES), lambda i: (i % tiles_per_seq, 0))

    return pl.pallas_call(
        _proj1_kernel,
        grid=(n // tm,),
        in_specs=[
            pl.BlockSpec((tm, D_MODEL), lambda i: (i, 0)),
            _resident((1, D_MODEL), lambda i: (0, 0)),
            modspec(0), modspec(1),
            _resident((D_MODEL, ncol), lambda i: (0, 0)),
            _resident((LANES, LANES), lambda i: (0, 0)),
            _resident((1, LANES), lambda i: (0, 0)),
            _resident((1, LANES), lambda i: (0, 0)),
            table(), table(), table(),
        ],
        out_specs=[
            pl.BlockSpec((None, N_PAIRS, LANES, tm), lambda i: (i // tiles_per_seq, 0, 0, i % tiles_per_seq)),
            pl.BlockSpec((tm, kvw), lambda i: (i, 0)),
            pl.BlockSpec((None, N_KV_HEADS, 1, VT_ROWS, tm),
                         lambda i: (i // tiles_per_seq, 0, i % tiles_per_seq, 0, 0)),
        ],
        out_shape=[
            jax.ShapeDtypeStruct((batch, N_PAIRS, LANES, seq), BF16),
            jax.ShapeDtypeStruct((n, kvw), BF16),
            jax.ShapeDtypeStruct((batch, N_KV_HEADS, tiles_per_seq, VT_ROWS, tm), BF16),
        ],
        compiler_params=_params(1),
        name="l1_qkv_rope",
    )(x2d, g, mod_r, mod_r, w, jnp.asarray(m2, BF16), qg2, kg2, cos, sn, sp)


def _attn1_kernel(qt_ref, k_ref, vt_ref, o_ref, w_scr, st_scr, m_scr, acc_scr, *, tq, tk, seq):
    row = lax.broadcasted_iota(jnp.int32, (LANES, tq), 0)
    top = row < HEAD_DIM
    for h in range(Q_PER_KV):
        blk = qt_ref[h // 2]
        keep = top if h % 2 == 0 else jnp.logical_not(top)
        w_scr[h] = jnp.where(keep, blk, jnp.zeros_like(blk))
    m_scr[...] = jnp.full(m_scr.shape, NEG_BIG, F32)
    acc_scr[...] = jnp.zeros(acc_scr.shape, F32)
    nk = seq // tk

    def scores(h, kt, buf):
        start = pl.multiple_of(kt * tk, tk)
        st_scr[buf, h] = jnp.dot(k_ref[pl.ds(start, tk), :], w_scr[h], preferred_element_type=F32)

    def softmax_pv(h, kt, buf):
        st = st_scr[buf, h]
        m_prev = m_scr[h]
        m_new = jnp.maximum(m_prev, jnp.max(st, axis=0, keepdims=True))
        p = jnp.exp(st - m_new).astype(BF16)
        acc_scr[h] = jnp.exp(m_prev - m_new) * acc_scr[h] + jnp.dot(vt_ref[kt], p, preferred_element_type=F32)
        m_scr[h] = m_new

    def step(kt, cur, nxt):
        scores(0, kt + 1, nxt)
        scores(1, kt + 1, nxt)
        softmax_pv(0, kt, cur)
        scores(2, kt + 1, nxt)
        softmax_pv(1, kt, cur)
        scores(3, kt + 1, nxt)
        softmax_pv(2, kt, cur)
        softmax_pv(3, kt, cur)

    for h in range(Q_PER_KV):
        scores(h, 0, 0)

    def body(i, carry):
        step(2 * i, 0, 1)
        step(2 * i + 1, 1, 0)
        return carry

    lax.fori_loop(0, nk // 2 - 1, body, 0)
    step(nk - 2, 0, 1)
    for h in range(Q_PER_KV):
        softmax_pv(h, nk - 1, 1)
    for g2 in range(Q_PER_KV // 2):
        parts = []
        for hh in range(2):
            a = acc_scr[2 * g2 + hh]
            parts.append(a[:HEAD_DIM] / a[HEAD_DIM:HEAD_DIM + 1])
        o_ref[:, g2 * LANES:(g2 + 1) * LANES] = jnp.concatenate(parts, axis=0).T.astype(BF16)


def _attn1(qt, kd, vt, batch, seq):
    tq, tk = 512, ATT1_TK
    qtiles = seq // tq
    qw = Q_PER_KV * HEAD_DIM
    kern = functools.partial(_attn1_kernel, tq=tq, tk=tk, seq=seq)
    return pl.pallas_call(
        kern,
        grid=(batch, N_KV_HEADS, qtiles),
        in_specs=[
            pl.BlockSpec((None, Q_PER_KV // 2, LANES, tq), lambda b, j, i: (b, j, 0, i)),
            pl.BlockSpec((seq, LANES), lambda b, j, i: (b, j)),
            pl.BlockSpec((None, None, seq // tk, VT_ROWS, tk), lambda b, j, i: (b, j, 0, 0, 0)),
        ],
        out_specs=pl.BlockSpec((tq, qw), lambda b, j, i: (b * qtiles + i, j)),
        out_shape=jax.ShapeDtypeStruct((batch * seq, D_MODEL), BF16),
        scratch_shapes=[
            pltpu.VMEM((Q_PER_KV, LANES, tq), BF16),
            pltpu.VMEM((2, Q_PER_KV, tk, tq), F32),
            pltpu.VMEM((Q_PER_KV, 1, tq), F32),
            pltpu.VMEM((Q_PER_KV, VT_ROWS, tq), F32),
        ],
        compiler_params=_params(3),
        name="l1_attn",
    )(qt, kd, vt)


TOP_K = 2
ROW_SUB = D_MODEL // LANES
EXPERT_TILE = 512
ROUTE_TILE = 512
MOVE_TILE = 512


def _store_token_major(ref, val):
    tm = val.shape[0]
    for s in range(ROW_SUB):
        ref[pl.ds(s, tm, stride=ROW_SUB), :] = val[:, s * LANES:(s + 1) * LANES]


def _load_token_major(ref):
    tm = ref.shape[0] // ROW_SUB
    return jnp.concatenate([ref[pl.ds(s, tm, stride=ROW_SUB), :] for s in range(ROW_SUB)], axis=-1)


def _token_rows(ref, t):
    return ref.at[pl.ds(pl.multiple_of(t * ROW_SUB, ROW_SUB), ROW_SUB)]


def _route_kernel(x_ref, g_ref, sh_ref, sc_ref, rw_ref, tri_ref, hp_ref, idx_ref, gates_ref, rank_ref,
                  counts_ref, run_scr):
    i = pl.program_id(0)
    tm = x_ref.shape[0]
    h = _prenorm(x_ref[...], g_ref[...], sh_ref[...], sc_ref[...])
    _store_token_major(hp_ref, h)
    logits = jnp.dot(h, rw_ref[...], preferred_element_type=F32, precision=lax.Precision.HIGHEST)
    lt = logits.T[:N_EXPERTS]
    sub = lax.broadcasted_iota(jnp.int32, (N_EXPERTS, tm), 0)
    m1 = jnp.max(lt, axis=0, keepdims=True)
    i1 = jnp.min(jnp.where(lt == m1, sub, N_EXPERTS), axis=0, keepdims=True)
    rest = jnp.where(sub == i1, NEG_BIG, lt)
    m2 = jnp.max(rest, axis=0, keepdims=True)
    i2 = jnp.min(jnp.where(rest == m2, sub, N_EXPERTS), axis=0, keepdims=True)
    t = jnp.exp(m2 - m1)
    gates_ref[0:1, :] = 1.0 / (1.0 + t)
    gates_ref[1:2, :] = t / (1.0 + t)
    idx_ref[0:1, :] = i1
    idx_ref[1:2, :] = i2

    @pl.when(i == 0)
    def _init():
        run_scr[...] = jnp.zeros(run_scr.shape, F32)

    onehot = jnp.where((sub == i1) | (sub == i2), 1.0, 0.0)
    before = jnp.dot(onehot.astype(BF16), tri_ref[...], preferred_element_type=F32) + run_scr[:, 0:1]
    rank_ref[0:1, :] = jnp.sum(jnp.where(sub == i1, before, 0.0), axis=0, keepdims=True).astype(jnp.int32)
    rank_ref[1:2, :] = jnp.sum(jnp.where(sub == i2, before, 0.0), axis=0, keepdims=True).astype(jnp.int32)
    run_scr[...] = run_scr[...] + jnp.sum(onehot, axis=1, keepdims=True)
    counts_ref[...] = run_scr[...]


def _route(x2d, mod_r, mod_base, g_pre, router_w, seq):
    n = x2d.shape[0]
    tm = ROUTE_TILE
    tiles_per_seq = seq // tm
    rw = jnp.zeros((D_MODEL, LANES), F32).at[:, :N_EXPERTS].set(router_w.astype(F32))
    tri = jnp.asarray(np.triu(np.ones((tm, tm), np.float32), k=1), BF16)

    def modspec(k):
        return pl.BlockSpec((None, 1, D_MODEL), lambda i: (mod_base + (i // tiles_per_seq) * N_MOD + k, 0, 0))

    def per_token():
        return pl.BlockSpec((TOP_K, tm), lambda i: (0, i))

    return pl.pallas_call(
        _route_kernel,
        grid=(n // tm,),
        in_specs=[
            pl.BlockSpec((tm, D_MODEL), lambda i: (i, 0)),
            _resident((1, D_MODEL), lambda i: (0, 0)),
            modspec(3), modspec(4),
            _resident((D_MODEL, LANES), lambda i: (0, 0)),
            _resident((tm, tm), lambda i: (0, 0)),
        ],
        out_specs=[
            pl.BlockSpec((tm * ROW_SUB, LANES), lambda i: (i, 0)),
            per_token(), per_token(), per_token(),
            pl.BlockSpec((N_EXPERTS, LANES), lambda i: (0, 0)),
        ],
        out_shape=[
            jax.ShapeDtypeStruct((n * ROW_SUB, LANES), F32),
            jax.ShapeDtypeStruct((TOP_K, n), jnp.int32),
            jax.ShapeDtypeStruct((TOP_K, n), F32),
            jax.ShapeDtypeStruct((TOP_K, n), jnp.int32),
            jax.ShapeDtypeStruct((N_EXPERTS, LANES), F32),
        ],
        scratch_shapes=[pltpu.VMEM((N_EXPERTS, LANES), F32)],
        compiler_params=_params(1, n_arbitrary=1),
        name="l1_route",
    )(x2d, g_pre, mod_r, mod_r, rw, tri)


def _token_copies_wait(src_ref, dst_ref, sem, tokens):
    rows = tokens * ROW_SUB
    pltpu.make_async_copy(src_ref.at[pl.ds(0, rows)], dst_ref.at[pl.ds(0, rows)], sem).wait()


def _dispatch_kernel(offs_ref, idx_ref, rank_ref, hp_ref, zeros_ref, xs_ref, sem):
    del zeros_ref
    tm = hp_ref.shape[0] // ROW_SUB

    def issue(t, carry):
        for k in range(TOP_K):
            pos = offs_ref[idx_ref[k, t]] + rank_ref[k, t]
            pltpu.make_async_copy(_token_rows(hp_ref, t), _token_rows(xs_ref, pos), sem).start()
        return carry

    lax.fori_loop(0, tm, issue, 0)
    for k in range(TOP_K):
        _token_copies_wait(hp_ref, xs_ref, sem, tm)


def _dispatch(offs, idx, rank, hp, n_rows):
    n = hp.shape[0] // ROW_SUB
    tm = MOVE_TILE
    zeros = jnp.zeros((n_rows * ROW_SUB, LANES), F32)
    smem = functools.partial(pl.BlockSpec, memory_space=pltpu.SMEM)
    return pl.pallas_call(
        _dispatch_kernel,
        grid_spec=pltpu.PrefetchScalarGridSpec(
            num_scalar_prefetch=1,
            grid=(n // tm,),
            in_specs=[
                smem((TOP_K, tm), lambda i, offs: (0, i)),
                smem((TOP_K, tm), lambda i, offs: (0, i)),
                pl.BlockSpec((tm * ROW_SUB, LANES), lambda i, offs: (i, 0)),
                pl.BlockSpec(memory_space=pl.ANY),
            ],
            out_specs=pl.BlockSpec(memory_space=pl.ANY),
            scratch_shapes=[pltpu.SemaphoreType.DMA(())],
        ),
        out_shape=jax.ShapeDtypeStruct((n_rows * ROW_SUB, LANES), F32),
        input_output_aliases={4: 0},
        compiler_params=_params(1, n_arbitrary=1),
        name="l1_dispatch",
    )(offs, idx, rank, hp, zeros)


def _expert_ffn_kernel(tile_expert_ref, n_valid_ref, xs_ref, wg_ref, wu_ref, wd_ref, ys_ref):
    j = pl.program_id(0)

    @pl.when(j < n_valid_ref[0])
    def _compute():
        h = _load_token_major(xs_ref).astype(BF16)
        y = _swiglu_acc(h, wg_ref.at[0], wu_ref.at[0], wd_ref.at[0], jnp.zeros((h.shape[0], D_MODEL), F32))
        _store_token_major(ys_ref, y)

    @pl.when(j >= n_valid_ref[0])
    def _skip():
        ys_ref[...] = jnp.zeros(ys_ref.shape, F32)


def _expert_ffn(tile_expert, n_valid, xs, wg, wu, wd):
    n_rows = xs.shape[0] // ROW_SUB
    tm = EXPERT_TILE

    def row_tile(j, te, nv):
        return (jnp.minimum(j, nv[0] - 1), 0)

    return pl.pallas_call(
        _expert_ffn_kernel,
        grid_spec=pltpu.PrefetchScalarGridSpec(
            num_scalar_prefetch=2,
            grid=(n_rows // tm,),
            in_specs=[
                pl.BlockSpec((tm * ROW_SUB, LANES), row_tile),
                pl.BlockSpec((1, D_MODEL, D_FF), lambda j, te, nv: (te[j], 0, 0)),
                pl.BlockSpec((1, D_MODEL, D_FF), lambda j, te, nv: (te[j], 0, 0)),
                pl.BlockSpec((1, D_FF, D_MODEL), lambda j, te, nv: (te[j], 0, 0)),
            ],
            out_specs=pl.BlockSpec((tm * ROW_SUB, LANES), lambda j, te, nv: (j, 0)),
        ),
        out_shape=jax.ShapeDtypeStruct((n_rows * ROW_SUB, LANES), F32),
        compiler_params=_params(1, n_arbitrary=1),
        name="l1_expert_ffn",
    )(tile_expert, n_valid, xs, wg, wu, wd)


def _combine_kernel(offs_ref, idx_ref, rank_ref, gates_ref, x_ref, ys_ref, g2_ref, gate_ref, out_ref,
                    ybuf, sem):
    tm = x_ref.shape[0]

    def issue(t, carry):
        for k in range(TOP_K):
            pos = offs_ref[idx_ref[k, t]] + rank_ref[k, t]
            pltpu.make_async_copy(_token_rows(ys_ref, pos), _token_rows(ybuf.at[k], t), sem).start()
        return carry

    lax.fori_loop(0, tm, issue, 0)
    pad = jnp.zeros((8 - TOP_K, tm), F32)
    gcols = jnp.concatenate([gates_ref[...], pad], axis=0).T
    for k in range(TOP_K):
        _token_copies_wait(ys_ref, ybuf.at[k], sem, tm)
    y = (gcols[:, 0:1] * _load_token_major(ybuf.at[0])
         + gcols[:, 1:2] * _load_token_major(ybuf.at[1]))
    out_ref[...] = x_ref[...] + gate_ref[...] * _rms(y, g2_ref[...])


def _combine(offs, idx, rank, gates, x2d, ys, mod_r, mod_base, g_post, seq):
    n = x2d.shape[0]
    tm = MOVE_TILE
    tiles_per_seq = seq // tm
    smem = functools.partial(pl.BlockSpec, memory_space=pltpu.SMEM)
    return pl.pallas_call(
        _combine_kernel,
        grid_spec=pltpu.PrefetchScalarGridSpec(
            num_scalar_prefetch=1,
            grid=(n // tm,),
            in_specs=[
                smem((TOP_K, tm), lambda i, offs: (0, i)),
                smem((TOP_K, tm), lambda i, offs: (0, i)),
                pl.BlockSpec((TOP_K, tm), lambda i, offs: (0, i)),
                pl.BlockSpec((tm, D_MODEL), lambda i, offs: (i, 0)),
                pl.BlockSpec(memory_space=pl.ANY),
                _resident((1, D_MODEL), lambda i, offs: (0, 0)),
                pl.BlockSpec((None, 1, D_MODEL),
                             lambda i, offs: (mod_base + (i // tiles_per_seq) * N_MOD + 5, 0, 0)),
            ],
            out_specs=pl.BlockSpec((tm, D_MODEL), lambda i, offs: (i, 0)),
            scratch_shapes=[
                pltpu.VMEM((TOP_K, tm * ROW_SUB, LANES), F32),
                pltpu.SemaphoreType.DMA(()),
            ],
        ),
        out_shape=jax.ShapeDtypeStruct((n, D_MODEL), F32),
        compiler_params=_params(1, n_arbitrary=1),
        name="l1_combine",
    )(offs, idx, rank, gates, x2d, ys, g_post, mod_r)


def _moe(x2d, mod_r, mod_base, g_pre, g_post, router_w, wg, wu, wd, seq):
    n = x2d.shape[0]
    hp, idx, gates, rank, counts = _route(x2d, mod_r, mod_base, g_pre, router_w, seq)
    n_tiles = TOP_K * n // EXPERT_TILE + N_EXPERTS
    cnt = counts[:, 0].astype(jnp.int32)
    tiles = (cnt + EXPERT_TILE - 1) // EXPERT_TILE
    tile_end = jnp.cumsum(tiles)
    offs = (tile_end - tiles) * EXPERT_TILE
    n_valid = tile_end[-1:]
    tile_ids = jnp.arange(n_tiles, dtype=jnp.int32)
    tile_expert = jnp.minimum(jnp.sum(tile_ids[:, None] >= tile_end[None, :], axis=1), N_EXPERTS - 1)
    tile_expert = jnp.where(tile_ids < n_valid[0], tile_expert, tile_expert[jnp.maximum(n_valid[0] - 1, 0)])
    xs = _dispatch(offs, idx, rank, hp, n_tiles * EXPERT_TILE)
    ys = _expert_ffn(tile_expert.astype(jnp.int32), n_valid, xs, wg, wu, wd)
    return _combine(offs, idx, rank, gates, x2d, ys, mod_r, mod_base, g_post, seq)


def kernel(x, c, mod_w, mod_b, norm_g, l0_w_in, l0_w_out, l0_ffn_w_gate, l0_ffn_w_up, l0_ffn_w_down,
           l1_w_in, l1_q_norm_g, l1_k_norm_g, l1_w_out, l1_router_w, l1_exp_w_gate, l1_exp_w_up,
           l1_exp_w_down):
    batch, seq, d = x.shape
    assert d == D_MODEL and seq % 2048 == 0
    n = batch * seq
    x2d = x.reshape(n, d)
    mod = _mod_vectors(c, mod_w, mod_b)
    mod_r = mod.reshape(mod.shape[0] * batch * N_MOD, 1, d)
    gains = norm_g.reshape(norm_g.shape[0], norm_g.shape[1], 1, d)

    base0 = 0
    gw = 3 * d
    o_list, lse_list = [], []
    for gi, (_, dil) in enumerate(DILATED_GROUPS):
        w_g = l0_w_in[:, gi * gw:(gi + 1) * gw].astype(BF16)
        qkv = _proj0(x2d, mod_r, base0, gains[0, 0], w_g, dil, batch, seq)
        o_g, lse_g = _attn0(qkv, dil)
        o_list.append(o_g)
        lse_list.append(lse_g)
    x2d = _merge_out(o_list, lse_list, x2d, mod_r, base0, l0_w_out.astype(BF16), gains[0, 1], batch, seq)
    x2d = _ffn(x2d, mod_r, base0, gains[0, 2], gains[0, 3], l0_ffn_w_gate.astype(BF16),
               l0_ffn_w_up.astype(BF16), l0_ffn_w_down.astype(BF16), seq)

    base1 = batch * N_MOD
    qt, kd, vt = _proj1(x2d, mod_r, base1, gains[1, 0], l1_w_in.astype(BF16), l1_q_norm_g, l1_k_norm_g,
                        batch, seq)
    o = _attn1(qt, kd, vt, batch, seq)
    x2d = _out_proj(o, x2d, mod_r, base1, l1_w_out.astype(BF16), gains[1, 1], seq)
    x2d = _moe(x2d, mod_r, base1, gains[1, 2], gains[1, 3], l1_router_w, l1_exp_w_gate.astype(BF16),
               l1_exp_w_up.astype(BF16), l1_exp_w_down.astype(BF16), seq)
    return x2d.reshape(batch, seq, d)
```

```python
import functools
import math

import numpy as np
import jax
import jax.numpy as jnp
from jax import lax
from jax.experimental import pallas as pl
from jax.experimental.pallas import tpu as pltpu

F32 = jnp.float32
BF16 = jnp.bfloat16

D_MODEL = 1024
N_HEADS = 16
HEAD_DIM = 64
LANES = 128
N_PAIRS = D_MODEL // LANES
DILATED_GROUPS = ((128, 1), (512, 4), (2048, 16))
N_SIDE = 64
N_KV_HEADS = 4
Q_PER_KV = N_HEADS // N_KV_HEADS
GRID_W = 64
ROPE_THETA = 10000.0
D_FF = 3584
N_EXPERTS = 8
N_MOD = 6
EPS = 1e-6
NEG_BIG = -1e30
VMEM_LIMIT_BYTES = 56 * 1024 * 1024


def _params(n_axes, n_arbitrary=0):
    sem = ("parallel",) * (n_axes - n_arbitrary) + ("arbitrary",) * n_arbitrary
    return pltpu.CompilerParams(dimension_semantics=sem, vmem_limit_bytes=VMEM_LIMIT_BYTES)


def _resident(shape, index_map):
    return pl.BlockSpec(shape, index_map, pipeline_mode=pl.Buffered(1))


def _rms(x, g):
    ms = jnp.mean(x * x, axis=-1, keepdims=True)
    return x * lax.rsqrt(ms + EPS) * g


def _prenorm(x, g, shift, scale):
    return _rms(x, g) * (1.0 + scale) + shift


def _sigmoid(x):
    return 1.0 / (1.0 + jnp.exp(-x))


def _mod_kernel(c_ref, w_ref, b_ref, o_ref):
    c = c_ref[...]
    ca = c * _sigmoid(c)
    o_ref[0] = jnp.dot(ca, w_ref[0], preferred_element_type=F32, precision=lax.Precision.HIGHEST) + b_ref[0]


def _mod_vectors(c, mod_w, mod_b):
    depth, d, n6 = mod_w.shape
    b = c.shape[0]
    tn = 1536
    return pl.pallas_call(
        _mod_kernel,
        grid=(depth, n6 // tn),
        in_specs=[
            pl.BlockSpec((b, d), lambda l, j: (0, 0)),
            pl.BlockSpec((1, d, tn), lambda l, j: (l, 0, j)),
            pl.BlockSpec((1, 1, tn), lambda l, j: (l, 0, j)),
        ],
        out_specs=pl.BlockSpec((1, b, tn), lambda l, j: (l, 0, j)),
        out_shape=jax.ShapeDtypeStruct((depth, b, n6), F32),
        compiler_params=_params(2),
        name="mod_vectors",
    )(c, mod_w, mod_b.reshape(depth, 1, n6))


PERM_SUB = 256


def _perm_matrix(d):
    p = np.zeros((PERM_SUB, PERM_SUB), np.float32)
    j = np.arange(PERM_SUB)
    p[(j % d) * (PERM_SUB // d) + j // d, j] = 1.0
    return p


def _proj0_kernel(x_ref, g_ref, sh_ref, sc_ref, p_ref, w_ref, o_ref, h_scr, *, d, tm, cw):
    h = _prenorm(x_ref[...], g_ref[...], sh_ref[...], sc_ref[...]).astype(BF16)
    per = tm // d
    if d == 1:
        h_scr[...] = h
    else:
        sub_per = PERM_SUB // d
        for s in range(tm // PERM_SUB):
            hs = jnp.dot(p_ref[...], h[s * PERM_SUB:(s + 1) * PERM_SUB], preferred_element_type=F32).astype(BF16)
            for r in range(d):
                h_scr[r * per + s * sub_per:r * per + (s + 1) * sub_per, :] = hs[r * sub_per:(r + 1) * sub_per]
    hp = h_scr[...]
    for j in range(w_ref.shape[1] // cw):
        res = jnp.dot(hp, w_ref[:, j * cw:(j + 1) * cw], preferred_element_type=F32).astype(BF16)
        for r in range(d):
            o_ref[r, :, j * cw:(j + 1) * cw] = res[r * per:(r + 1) * per]


def _proj0(x2d, mod_r, mod_base, g, w, d, batch, seq):
    n = x2d.shape[0]
    tm = 512
    tiles_per_seq = seq // tm
    ncol = w.shape[1]
    kern = functools.partial(_proj0_kernel, d=d, tm=tm, cw=512)
    perm = jnp.asarray(_perm_matrix(d), BF16)
    return pl.pallas_call(
        kern,
        grid=(n // tm,),
        in_specs=[
            pl.BlockSpec((tm, D_MODEL), lambda i: (i, 0)),
            _resident((1, D_MODEL), lambda i: (0, 0)),
            pl.BlockSpec((None, 1, D_MODEL), lambda i: (mod_base + (i // tiles_per_seq) * N_MOD + 0, 0, 0)),
            pl.BlockSpec((None, 1, D_MODEL), lambda i: (mod_base + (i // tiles_per_seq) * N_MOD + 1, 0, 0)),
            _resident((PERM_SUB, PERM_SUB), lambda i: (0, 0)),
            _resident((D_MODEL, ncol), lambda i: (0, 0)),
        ],
        out_specs=pl.BlockSpec((None, d, tm // d, ncol), lambda i: (i // tiles_per_seq, 0, i % tiles_per_seq, 0)),
        out_shape=jax.ShapeDtypeStruct((batch, d, seq // d, ncol), BF16),
        scratch_shapes=[pltpu.VMEM((tm, D_MODEL), BF16)],
        compiler_params=_params(1),
        name=f"l0_qkv_d{d}",
    )(x2d, g, mod_r, mod_r, perm, w)


ATT0_SQ = 128
ATT0_TK = ATT0_SQ + 2 * N_SIDE


def _attn0_kernel(q_ref, kp_ref, kc_ref, kn_ref, vp_ref, vc_ref, vn_ref, o_ref, lse_ref, kbuf, vbuf,
                  *, d, tq, length):
    i = pl.program_id(2)
    kbuf[0:N_SIDE] = kp_ref[...]
    kbuf[N_SIDE:N_SIDE + tq] = kc_ref[...]
    kbuf[N_SIDE + tq:] = kn_ref[...]
    vbuf[0:N_SIDE] = vp_ref[...]
    vbuf[N_SIDE:N_SIDE + tq] = vc_ref[...]
    vbuf[N_SIDE + tq:] = vn_ref[...]

    lane = lax.broadcasted_iota(jnp.int32, (ATT0_SQ, LANES), 1)
    low = lane < HEAD_DIM
    a_idx = lax.broadcasted_iota(jnp.int32, (ATT0_SQ, ATT0_TK), 0)
    c_idx = lax.broadcasted_iota(jnp.int32, (ATT0_SQ, ATT0_TK), 1)
    absrel = jnp.abs(c_idx - N_SIDE - a_idx)
    for j in range(tq // ATT0_SQ):
        u_key = i * tq + (j * ATT0_SQ - N_SIDE) + c_idx
        valid = (absrel <= N_SIDE) & (u_key >= 0) & (u_key < length)
        base = jnp.where(valid, (-float(d)) * absrel.astype(F32), NEG_BIG)
        lse_tile = jnp.zeros((ATT0_SQ, LANES), F32)
        for p in range(N_PAIRS):
            cols = slice(p * LANES, (p + 1) * LANES)
            q2 = q_ref[j * ATT0_SQ:(j + 1) * ATT0_SQ, cols]
            zero = jnp.zeros_like(q2)
            qs = jnp.concatenate([jnp.where(low, q2, zero), jnp.where(low, zero, q2)], axis=0)
            k2 = kbuf[j * ATT0_SQ:j * ATT0_SQ + ATT0_TK, cols]
            v2 = vbuf[j * ATT0_SQ:j * ATT0_SQ + ATT0_TK, cols]
            s = lax.dot_general(qs, k2, (((1,), (1,)), ((), ())), preferred_element_type=F32)
            s = s * (HEAD_DIM ** -0.5)
            ps, ls = [], []
            for hh in range(2):
                head = 2 * p + hh
                slope = 2.0 ** (-8.0 * (head + 1) / N_HEADS)
                sh = s[hh * ATT0_SQ:(hh + 1) * ATT0_SQ] + slope * base
                m = jnp.max(sh, axis=-1, keepdims=True)
                e = jnp.exp(sh - m)
                l = jnp.sum(e, axis=-1, keepdims=True)
                ps.append(e.astype(BF16))
                ls.append(l)
                lse_tile = jnp.where(lane == head, m + jnp.log(l), lse_tile)
            o = jnp.dot(jnp.concatenate(ps, axis=0), v2, preferred_element_type=F32)
            o2 = jnp.where(low, o[:ATT0_SQ] / ls[0], o[ATT0_SQ:] / ls[1])
            o_ref[j * ATT0_SQ:(j + 1) * ATT0_SQ, cols] = o2.astype(BF16)
        lse_ref[j * ATT0_SQ:(j + 1) * ATT0_SQ, :] = lse_tile


def _attn0(qkv, d):
    batch, _, length, _ = qkv.shape
    tq = min(512, length)
    nb = tq // N_SIDE
    last = length // N_SIDE - 1
    kern = functools.partial(_attn0_kernel, d=d, tq=tq, length=length)

    def main(col):
        return pl.BlockSpec((None, None, tq, D_MODEL), lambda b, r, i: (b, r, i, col))

    def prev(col):
        return pl.BlockSpec((None, None, N_SIDE, D_MODEL), lambda b, r, i: (b, r, jnp.maximum(i * nb - 1, 0), col))

    def nxt(col):
        return pl.BlockSpec((None, None, N_SIDE, D_MODEL), lambda b, r, i: (b, r, jnp.minimum((i + 1) * nb, last), col))

    return pl.pallas_call(
        kern,
        grid=(batch, d, length // tq),
        in_specs=[main(0), prev(1), main(1), nxt(1), prev(2), main(2), nxt(2)],
        out_specs=[
            pl.BlockSpec((None, None, tq, D_MODEL), lambda b, r, i: (b, r, i, 0)),
            pl.BlockSpec((None, None, tq, LANES), lambda b, r, i: (b, r, i, 0)),
        ],
        out_shape=[
            jax.ShapeDtypeStruct((batch, d, length, D_MODEL), BF16),
            jax.ShapeDtypeStruct((batch, d, length, LANES), F32),
        ],
        scratch_shapes=[
            pltpu.VMEM((tq + 2 * N_SIDE, D_MODEL), BF16),
            pltpu.VMEM((tq + 2 * N_SIDE, D_MODEL), BF16),
        ],
        compiler_params=_params(3),
        name=f"l0_attn_d{d}",
    )(qkv, qkv, qkv, qkv, qkv, qkv, qkv)


def _split3(x):
    hi = x.astype(BF16)
    r1 = x - hi.astype(F32)
    mid = r1.astype(BF16)
    lo = (r1 - mid.astype(F32)).astype(BF16)
    return hi, mid, lo


def _unpermute(pt, blk_ref, d, tm, exact_f32):
    sub_per = PERM_SUB // d
    outs = []
    for s in range(tm // PERM_SUB):
        src = jnp.concatenate([blk_ref[r, s * sub_per:(s + 1) * sub_per, :] for r in range(d)], axis=0)
        if exact_f32:
            parts = _split3(src)
            nat = sum(jnp.dot(pt, part, preferred_element_type=F32) for part in parts)
        else:
            nat = jnp.dot(pt, src, preferred_element_type=F32)
        outs.append(nat)
    return jnp.concatenate(outs, axis=0)


def _merge_out_kernel(o0_ref, o1_ref, o2_ref, l0_ref, l1_ref, l2_ref, p1_ref, p2_ref, e_ref,
                      x_ref, w_ref, g_ref, gate_ref, out_ref, *, tm):
    dils = [dil for _, dil in DILATED_GROUPS]
    o_refs = (o0_ref, o1_ref, o2_ref)
    l_refs = (l0_ref, l1_ref, l2_ref)
    pts = (None, p1_ref[...], p2_ref[...])
    os_, ls_ = [], []
    for gi, d in enumerate(dils):
        if d == 1:
            os_.append(o_refs[gi][0].astype(F32))
            ls_.append(l_refs[gi][0])
        else:
            os_.append(_unpermute(pts[gi], o_refs[gi], d, tm, False))
            ls_.append(_unpermute(pts[gi], l_refs[gi], d, tm, True))
    m = jnp.maximum(jnp.maximum(ls_[0], ls_[1]), ls_[2])
    es = [jnp.exp(l - m) for l in ls_]
    den = es[0] + es[1] + es[2]
    acc = jnp.zeros((tm, D_MODEL), F32)
    for gi in range(3):
        wgt = es[gi] / den
        hi, mid, lo = _split3(wgt)
        wfull = (jnp.dot(hi, e_ref[...], preferred_element_type=F32)
                 + jnp.dot(mid, e_ref[...], preferred_element_type=F32)
                 + jnp.dot(lo, e_ref[...], preferred_element_type=F32))
        acc = acc + wfull * os_[gi]
    y = jnp.dot(acc.astype(BF16), w_ref[...], preferred_element_type=F32)
    out_ref[...] = x_ref[...] + gate_ref[...] * _rms(y, g_ref[...])


def _merge_out(o_list, lse_list, x2d, mod_r, mod_base, w_out, g, batch, seq):
    n = x2d.shape[0]
    tm = 512
    tiles_per_seq = seq // tm
    kern = functools.partial(_merge_out_kernel, tm=tm)
    dils = [dil for _, dil in DILATED_GROUPS]
    expand = np.zeros((LANES, D_MODEL), np.float32)
    for h in range(N_HEADS):
        expand[h, h * HEAD_DIM:(h + 1) * HEAD_DIM] = 1.0

    def grouped(d, w):
        return pl.BlockSpec((None, d, tm // d, w), lambda i: (i // tiles_per_seq, 0, i % tiles_per_seq, 0))

    in_specs = ([grouped(d, D_MODEL) for d in dils] + [grouped(d, LANES) for d in dils] + [
        _resident((PERM_SUB, PERM_SUB), lambda i: (0, 0)),
        _resident((PERM_SUB, PERM_SUB), lambda i: (0, 0)),
        _resident((LANES, D_MODEL), lambda i: (0, 0)),
        pl.BlockSpec((tm, D_MODEL), lambda i: (i, 0)),
        _resident((D_MODEL, D_MODEL), lambda i: (0, 0)),
        _resident((1, D_MODEL), lambda i: (0, 0)),
        pl.BlockSpec((None, 1, D_MODEL), lambda i: (mod_base + (i // tiles_per_seq) * N_MOD + 2, 0, 0)),
    ])
    return pl.pallas_call(
        kern,
        grid=(n // tm,),
        in_specs=in_specs,
        out_specs=pl.BlockSpec((tm, D_MODEL), lambda i: (i, 0)),
        out_shape=jax.ShapeDtypeStruct((n, D_MODEL), F32),
        compiler_params=_params(1),
        name="l0_merge_out",
    )(*o_list, *lse_list,
      jnp.asarray(_perm_matrix(dils[1]).T, BF16), jnp.asarray(_perm_matrix(dils[2]).T, BF16),
      jnp.asarray(expand, BF16), x2d, w_out, g, mod_r)


def _out_kernel(o_ref, x_ref, w_ref, g_ref, gate_ref, out_ref):
    y = jnp.dot(o_ref[...], w_ref[...], preferred_element_type=F32)
    out_ref[...] = x_ref[...] + gate_ref[...] * _rms(y, g_ref[...])


def _out_proj(o2d, x2d, mod_r, mod_base, w_out, g, seq):
    n = x2d.shape[0]
    tm = 512
    tiles_per_seq = seq // tm
    return pl.pallas_call(
        _out_kernel,
        grid=(n // tm,),
        in_specs=[
            pl.BlockSpec((tm, D_MODEL), lambda i: (i, 0)),
            pl.BlockSpec((tm, D_MODEL), lambda i: (i, 0)),
            _resident((D_MODEL, D_MODEL), lambda i: (0, 0)),
            _resident((1, D_MODEL), lambda i: (0, 0)),
            pl.BlockSpec((None, 1, D_MODEL), lambda i: (mod_base + (i // tiles_per_seq) * N_MOD + 2, 0, 0)),
        ],
        out_specs=pl.BlockSpec((tm, D_MODEL), lambda i: (i, 0)),
        out_shape=jax.ShapeDtypeStruct((n, D_MODEL), F32),
        compiler_params=_params(1),
        name="l1_out_proj",
    )(o2d, x2d, w_out, g, mod_r)


FF_CHUNK = 512


def _swiglu_acc(h, wg_ref, wu_ref, wd_ref, acc):
    for c in range(wg_ref.shape[-1] // FF_CHUNK):
        cols = slice(c * FF_CHUNK, (c + 1) * FF_CHUNK)
        a = jnp.dot(h, wg_ref[:, cols], preferred_element_type=F32)
        u = jnp.dot(h, wu_ref[:, cols], preferred_element_type=F32)
        t = (a * _sigmoid(a) * u).astype(BF16)
        acc = acc + jnp.dot(t, wd_ref[cols, :], preferred_element_type=F32)
    return acc


def _ffn_kernel(x_ref, g_ref, sh_ref, sc_ref, wg_ref, wu_ref, wd_ref, g2_ref, gate_ref, out_ref):
    x = x_ref[...]
    h = _prenorm(x, g_ref[...], sh_ref[...], sc_ref[...]).astype(BF16)
    y = _swiglu_acc(h, wg_ref, wu_ref, wd_ref, jnp.zeros(x.shape, F32))
    out_ref[...] = x + gate_ref[...] * _rms(y, g2_ref[...])


def _ffn(x2d, mod_r, mod_base, g_pre, g_post, wg, wu, wd, seq):
    n = x2d.shape[0]
    tm = 512
    tiles_per_seq = seq // tm

    def modspec(k):
        return pl.BlockSpec((None, 1, D_MODEL), lambda i: (mod_base + (i // tiles_per_seq) * N_MOD + k, 0, 0))

    return pl.pallas_call(
        _ffn_kernel,
        grid=(n // tm,),
        in_specs=[
            pl.BlockSpec((tm, D_MODEL), lambda i: (i, 0)),
            _resident((1, D_MODEL), lambda i: (0, 0)),
            modspec(3), modspec(4),
            _resident((D_MODEL, D_FF), lambda i: (0, 0)),
            _resident((D_MODEL, D_FF), lambda i: (0, 0)),
            _resident((D_FF, D_MODEL), lambda i: (0, 0)),
            _resident((1, D_MODEL), lambda i: (0, 0)),
            modspec(5),
        ],
        out_specs=pl.BlockSpec((tm, D_MODEL), lambda i: (i, 0)),
        out_shape=jax.ShapeDtypeStruct((n, D_MODEL), F32),
        compiler_params=_params(1),
        name="l0_ffn",
    )(x2d, g_pre, mod_r, mod_r, wg, wu, wd, g_post, mod_r)


ATT1_TK = 512
VT_ROWS = HEAD_DIM + 16
LOG2_E = math.log2(math.e)


def _proj1_kernel(x_ref, g_ref, sh_ref, sc_ref, w_ref, m2_ref, qg_ref, kg_ref, cos_ref, sn_ref, sp_ref,
                  qt_ref, kd_ref, vt_ref):
    h = _prenorm(x_ref[...], g_ref[...], sh_ref[...], sc_ref[...]).astype(BF16)
    proj = jnp.dot(h, w_ref[...], preferred_element_type=F32)
    tm = proj.shape[0]
    lane = lax.broadcasted_iota(jnp.int32, (tm, LANES), 1)
    low = lane < HEAD_DIM
    top = lax.broadcasted_iota(jnp.int32, (LANES, tm), 0) < HEAD_DIM
    cos, sn, sp = cos_ref[...], sn_ref[...], sp_ref[...]
    n_q = D_MODEL // LANES
    n_k = N_KV_HEADS * HEAD_DIM // LANES
    for c in range(n_q + n_k):
        z = proj[:, c * LANES:(c + 1) * LANES]
        zz = z * z
        hi = zz.astype(BF16)
        lo = (zz - hi.astype(F32)).astype(BF16)
        ms = jnp.dot(hi, m2_ref[...], preferred_element_type=F32) + jnp.dot(lo, m2_ref[...], preferred_element_type=F32)
        gain = qg_ref[...] if c < n_q else kg_ref[...]
        zn = z * lax.rsqrt(ms + EPS) * gain
        zr = zn * cos + pltpu.roll(zn, LANES - 16, 1) * sn + pltpu.roll(zn, 16, 1) * sp
        if c < n_q:
            zt = (zr * (HEAD_DIM ** -0.5 * LOG2_E)).T.astype(BF16)
            zero = jnp.zeros_like(zt)
            qt_ref[2 * c, 0] = jnp.where(top, zt, zero)
            qt_ref[2 * c + 1, 0] = jnp.where(top, zero, zt)
        else:
            j = c - n_q
            sw = pltpu.roll(zr, HEAD_DIM, 1)
            kd_ref[:, (2 * j) * LANES:(2 * j + 1) * LANES] = jnp.where(low, zr, sw).astype(BF16)
            kd_ref[:, (2 * j + 1) * LANES:(2 * j + 2) * LANES] = jnp.where(low, sw, zr).astype(BF16)
    v_base = D_MODEL + N_KV_HEADS * HEAD_DIM
    ones = jnp.ones((VT_ROWS - HEAD_DIM, tm), BF16)
    for j in range(n_k):
        zt = proj[:, v_base + j * LANES:v_base + (j + 1) * LANES].T
        for hh in range(2):
            vt_ref[2 * j + hh, 0, 0:HEAD_DIM, :] = zt[hh * HEAD_DIM:(hh + 1) * HEAD_DIM].astype(BF16)
            vt_ref[2 * j + hh, 0, HEAD_DIM:, :] = ones


def _rope_tables(seq):
    lane = np.arange(LANES)
    dd = lane % HEAD_DIM
    blk = dd // (HEAD_DIM // 2)
    idx = dd % (HEAD_DIM // 2)
    half = HEAD_DIM // 4
    fi = idx % half
    first = idx < half
    freqs = jnp.asarray(ROPE_THETA, F32) ** (-jnp.arange(half, dtype=F32) / half)
    t = jnp.arange(seq)
    row = (t // GRID_W).astype(F32)
    col = (t % GRID_W).astype(F32)
    pos = jnp.where(jnp.asarray(blk == 0)[None, :], row[:, None], col[:, None])
    ang = pos * freqs[jnp.asarray(fi)][None, :]
    cos = jnp.cos(ang)
    sin = jnp.sin(ang)
    first = jnp.asarray(first)[None, :]
    return cos, jnp.where(first, -sin, 0.0), jnp.where(first, 0.0, sin)


def _proj1(x2d, mod_r, mod_base, g, w, qg, kg, batch, seq):
    n = x2d.shape[0]
    tm = ATT1_TK
    tiles_per_seq = seq // tm
    ncol = w.shape[1]
    kvw = N_KV_HEADS * LANES
    m2 = np.zeros((LANES, LANES), np.float32)
    m2[:HEAD_DIM, :HEAD_DIM] = 1.0 / HEAD_DIM
    m2[HEAD_DIM:, HEAD_DIM:] = 1.0 / HEAD_DIM
    cos, sn, sp = _rope_tables(seq)
    qg2 = jnp.tile(qg.astype(F32), 2).reshape(1, LANES)
    kg2 = jnp.tile(kg.astype(F32), 2).reshape(1, LANES)

    def modspec(k):
        return pl.BlockSpec((None, 1, D_MODEL), lambda i: (mod_base + (i // tiles_per_seq) * N_MOD + k, 0, 0))

    def table():
        return pl.BlockSpec((tm, LANES), lambda i: (i % tiles_per_seq, 0))

    return pl.pallas_call(
        _proj1_kernel,
        grid=(n // tm,),
        in_specs=[
            pl.BlockSpec((tm, D_MODEL), lambda i: (i, 0)),
            _resident((1, D_MODEL), lambda i: (0, 0)),
            modspec(0), modspec(1),
            _resident((D_MODEL, ncol), lambda i: (0, 0)),
            _resident((LANES, LANES), lambda i: (0, 0)),
            _resident((1, LANES), lambda i: (0, 0)),
            _resident((1, LANES), lambda i: (0, 0)),
            table(), table(), table(),
        ],
        out_specs=[
            pl.BlockSpec((None, N_HEADS, 1, LANES, tm), lambda i: (i // tiles_per_seq, 0, i % tiles_per_seq, 0, 0)),
            pl.BlockSpec((tm, kvw), lambda i: (i, 0)),
            pl.BlockSpec((None, N_KV_HEADS, 1, VT_ROWS, tm),
                         lambda i: (i // tiles_per_seq, 0, i % tiles_per_seq, 0, 0)),
        ],
        out_shape=[
            jax.ShapeDtypeStruct((batch, N_HEADS, tiles_per_seq, LANES, tm), BF16),
            jax.ShapeDtypeStruct((n, kvw), BF16),
            jax.ShapeDtypeStruct((batch, N_KV_HEADS, tiles_per_seq, VT_ROWS, tm), BF16),
        ],
        compiler_params=_params(1),
        name="l1_qkv_rope",
    )(x2d, g, mod_r, mod_r, w, jnp.asarray(m2, BF16), qg2, kg2, cos, sn, sp)


def _attn1_kernel(qt_ref, k_ref, vt_ref, o_ref, st_scr, m_scr, acc_scr, *, tq, tk, seq):
    nk = seq // tk
    nq = seq // tq
    total = nq * nk

    def reset_state():
        m_scr[...] = jnp.full(m_scr.shape, NEG_BIG, F32)
        acc_scr[...] = jnp.zeros(acc_scr.shape, F32)

    def scores(h, qi, kt, buf):
        start = pl.multiple_of(kt * tk, tk)
        st_scr[buf, h] = jnp.dot(k_ref[pl.ds(start, tk), :], qt_ref[h, qi], preferred_element_type=F32)

    def softmax_pv(h, kt, buf):
        st = st_scr[buf, h]
        m_prev = m_scr[h]
        m_new = jnp.maximum(m_prev, jnp.max(st, axis=0, keepdims=True))
        p = jnp.exp2(st - m_new).astype(BF16)
        acc_scr[h] = jnp.exp2(m_prev - m_new) * acc_scr[h] + jnp.dot(vt_ref[kt], p, preferred_element_type=F32)
        m_scr[h] = m_new

    def step(kt, qi_next, kt_next, cur, nxt):
        scores(0, qi_next, kt_next, nxt)
        scores(1, qi_next, kt_next, nxt)
        softmax_pv(0, kt, cur)
        scores(2, qi_next, kt_next, nxt)
        softmax_pv(1, kt, cur)
        scores(3, qi_next, kt_next, nxt)
        softmax_pv(2, kt, cur)
        softmax_pv(3, kt, cur)

    def finish_tile(qi):
        start = pl.multiple_of(qi * tq, tq)
        for g2 in range(Q_PER_KV // 2):
            parts = []
            for hh in range(2):
                a = acc_scr[2 * g2 + hh]
                parts.append(a[:HEAD_DIM] / a[HEAD_DIM:HEAD_DIM + 1])
            o_ref[pl.ds(start, tq), g2 * LANES:(g2 + 1) * LANES] = jnp.concatenate(parts, axis=0).T.astype(BF16)
        reset_state()

    reset_state()
    for h in range(Q_PER_KV):
        scores(h, 0, 0, 0)

    def tile_body(qi, carry):
        def pair_body(i, inner):
            kt = 2 * i
            step(kt, qi, kt + 1, 0, 1)
            c2 = jnp.minimum(qi * nk + kt + 2, total - 1)
            step(kt + 1, c2 // nk, c2 % nk, 1, 0)
            return inner

        lax.fori_loop(0, nk // 2, pair_body, 0)
        finish_tile(qi)
        return carry

    lax.fori_loop(0, nq, tile_body, 0)


def _attn1(qt, kd, vt, batch, seq):
    tq, tk = ATT1_TK, ATT1_TK
    qw = Q_PER_KV * HEAD_DIM
    assert (seq // tk) % 2 == 0
    kern = functools.partial(_attn1_kernel, tq=tq, tk=tk, seq=seq)
    return pl.pallas_call(
        kern,
        grid=(batch, N_KV_HEADS),
        in_specs=[
            pl.BlockSpec((None, Q_PER_KV, seq // tq, LANES, tq), lambda b, j: (b, j, 0, 0, 0)),
            pl.BlockSpec((seq, LANES), lambda b, j: (b, j)),
            pl.BlockSpec((None, None, seq // tk, VT_ROWS, tk), lambda b, j: (b, j, 0, 0, 0)),
        ],
        out_specs=pl.BlockSpec((seq, qw), lambda b, j: (b, j)),
        out_shape=jax.ShapeDtypeStruct((batch * seq, D_MODEL), BF16),
        scratch_shapes=[
            pltpu.VMEM((2, Q_PER_KV, tk, tq), F32),
            pltpu.VMEM((Q_PER_KV, 1, tq), F32),
            pltpu.VMEM((Q_PER_KV, VT_ROWS, tq), F32),
        ],
        compiler_params=_params(2),
        name="l1_attn",
    )(qt, kd, vt)


TOP_K = 2
ROW_SUB = D_MODEL // LANES
EXPERT_TILE = 512
ROUTE_TILE = 512
MOVE_TILE = 512


def _store_token_major(ref, val):
    tm = val.shape[0]
    for s in range(ROW_SUB):
        ref[pl.ds(s, tm, stride=ROW_SUB), :] = val[:, s * LANES:(s + 1) * LANES]


def _load_token_major(ref):
    tm = ref.shape[0] // ROW_SUB
    return jnp.concatenate([ref[pl.ds(s, tm, stride=ROW_SUB), :] for s in range(ROW_SUB)], axis=-1)


def _token_rows(ref, t):
    return ref.at[pl.ds(pl.multiple_of(t * ROW_SUB, ROW_SUB), ROW_SUB)]


def _route_kernel(x_ref, g_ref, sh_ref, sc_ref, rw_ref, tri_ref, hp_ref, idx_ref, gates_ref, rank_ref,
                  counts_ref, run_scr):
    i = pl.program_id(0)
    tm = x_ref.shape[0]
    h = _prenorm(x_ref[...], g_ref[...], sh_ref[...], sc_ref[...])
    _store_token_major(hp_ref, h)
    logits = jnp.dot(h, rw_ref[...], preferred_element_type=F32, precision=lax.Precision.HIGHEST)
    lt = logits.T[:N_EXPERTS]
    sub = lax.broadcasted_iota(jnp.int32, (N_EXPERTS, tm), 0)
    m1 = jnp.max(lt, axis=0, keepdims=True)
    i1 = jnp.min(jnp.where(lt == m1, sub, N_EXPERTS), axis=0, keepdims=True)
    rest = jnp.where(sub == i1, NEG_BIG, lt)
    m2 = jnp.max(rest, axis=0, keepdims=True)
    i2 = jnp.min(jnp.where(rest == m2, sub, N_EXPERTS), axis=0, keepdims=True)
    t = jnp.exp(m2 - m1)
    gates_ref[0:1, :] = 1.0 / (1.0 + t)
    gates_ref[1:2, :] = t / (1.0 + t)
    idx_ref[0:1, :] = i1
    idx_ref[1:2, :] = i2

    @pl.when(i == 0)
    def _init():
        run_scr[...] = jnp.zeros(run_scr.shape, F32)

    onehot = jnp.where((sub == i1) | (sub == i2), 1.0, 0.0)
    before = jnp.dot(onehot.astype(BF16), tri_ref[...], preferred_element_type=F32) + run_scr[:, 0:1]
    rank_ref[0:1, :] = jnp.sum(jnp.where(sub == i1, before, 0.0), axis=0, keepdims=True).astype(jnp.int32)
    rank_ref[1:2, :] = jnp.sum(jnp.where(sub == i2, before, 0.0), axis=0, keepdims=True).astype(jnp.int32)
    run_scr[...] = run_scr[...] + jnp.sum(onehot, axis=1, keepdims=True)
    counts_ref[...] = run_scr[...]


def _route(x2d, mod_r, mod_base, g_pre, router_w, seq):
    n = x2d.shape[0]
    tm = ROUTE_TILE
    tiles_per_seq = seq // tm
    rw = jnp.zeros((D_MODEL, LANES), F32).at[:, :N_EXPERTS].set(router_w.astype(F32))
    tri = jnp.asarray(np.triu(np.ones((tm, tm), np.float32), k=1), BF16)

    def modspec(k):
        return pl.BlockSpec((None, 1, D_MODEL), lambda i: (mod_base + (i // tiles_per_seq) * N_MOD + k, 0, 0))

    def per_token():
        return pl.BlockSpec((TOP_K, tm), lambda i: (0, i))

    return pl.pallas_call(
        _route_kernel,
        grid=(n // tm,),
        in_specs=[
            pl.BlockSpec((tm, D_MODEL), lambda i: (i, 0)),
            _resident((1, D_MODEL), lambda i: (0, 0)),
            modspec(3), modspec(4),
            _resident((D_MODEL, LANES), lambda i: (0, 0)),
            _resident((tm, tm), lambda i: (0, 0)),
        ],
        out_specs=[
            pl.BlockSpec((tm * ROW_SUB, LANES), lambda i: (i, 0)),
            per_token(), per_token(), per_token(),
            pl.BlockSpec((N_EXPERTS, LANES), lambda i: (0, 0)),
        ],
        out_shape=[
            jax.ShapeDtypeStruct((n * ROW_SUB, LANES), F32),
            jax.ShapeDtypeStruct((TOP_K, n), jnp.int32),
            jax.ShapeDtypeStruct((TOP_K, n), F32),
            jax.ShapeDtypeStruct((TOP_K, n), jnp.int32),
            jax.ShapeDtypeStruct((N_EXPERTS, LANES), F32),
        ],
        scratch_shapes=[pltpu.VMEM((N_EXPERTS, LANES), F32)],
        compiler_params=_params(1, n_arbitrary=1),
        name="l1_route",
    )(x2d, g_pre, mod_r, mod_r, rw, tri)


def _token_copies_wait(src_ref, dst_ref, sem, tokens):
    rows = tokens * ROW_SUB
    pltpu.make_async_copy(src_ref.at[pl.ds(0, rows)], dst_ref.at[pl.ds(0, rows)], sem).wait()


def _dispatch_kernel(offs_ref, idx_ref, rank_ref, hp_ref, zeros_ref, xs_ref, sem):
    del zeros_ref
    tm = hp_ref.shape[0] // ROW_SUB

    def issue(t, carry):
        for k in range(TOP_K):
            pos = offs_ref[idx_ref[k, t]] + rank_ref[k, t]
            pltpu.make_async_copy(_token_rows(hp_ref, t), _token_rows(xs_ref, pos), sem).start()
        return carry

    lax.fori_loop(0, tm, issue, 0)
    for k in range(TOP_K):
        _token_copies_wait(hp_ref, xs_ref, sem, tm)


def _dispatch(offs, idx, rank, hp, n_rows):
    n = hp.shape[0] // ROW_SUB
    tm = MOVE_TILE
    zeros = jnp.zeros((n_rows * ROW_SUB, LANES), F32)
    smem = functools.partial(pl.BlockSpec, memory_space=pltpu.SMEM)
    return pl.pallas_call(
        _dispatch_kernel,
        grid_spec=pltpu.PrefetchScalarGridSpec(
            num_scalar_prefetch=1,
            grid=(n // tm,),
            in_specs=[
                smem((TOP_K, tm), lambda i, offs: (0, i)),
                smem((TOP_K, tm), lambda i, offs: (0, i)),
                pl.BlockSpec((tm * ROW_SUB, LANES), lambda i, offs: (i, 0)),
                pl.BlockSpec(memory_space=pl.ANY),
            ],
            out_specs=pl.BlockSpec(memory_space=pl.ANY),
            scratch_shapes=[pltpu.SemaphoreType.DMA(())],
        ),
        out_shape=jax.ShapeDtypeStruct((n_rows * ROW_SUB, LANES), F32),
        input_output_aliases={4: 0},
        compiler_params=_params(1, n_arbitrary=1),
        name="l1_dispatch",
    )(offs, idx, rank, hp, zeros)


def _expert_ffn_kernel(tile_expert_ref, n_valid_ref, xs_ref, wg_ref, wu_ref, wd_ref, ys_ref):
    j = pl.program_id(0)

    @pl.when(j < n_valid_ref[0])
    def _compute():
        h = _load_token_major(xs_ref).astype(BF16)
        y = _swiglu_acc(h, wg_ref.at[0], wu_ref.at[0], wd_ref.at[0], jnp.zeros((h.shape[0], D_MODEL), F32))
        _store_token_major(ys_ref, y)

    @pl.when(j >= n_valid_ref[0])
    def _skip():
        ys_ref[...] = jnp.zeros(ys_ref.shape, F32)


def _expert_ffn(tile_expert, n_valid, xs, wg, wu, wd):
    n_rows = xs.shape[0] // ROW_SUB
    tm = EXPERT_TILE

    def row_tile(j, te, nv):
        return (jnp.minimum(j, nv[0] - 1), 0)

    return pl.pallas_call(
        _expert_ffn_kernel,
        grid_spec=pltpu.PrefetchScalarGridSpec(
            num_scalar_prefetch=2,
            grid=(n_rows // tm,),
            in_specs=[
                pl.BlockSpec((tm * ROW_SUB, LANES), row_tile),
                pl.BlockSpec((1, D_MODEL, D_FF), lambda j, te, nv: (te[j], 0, 0)),
                pl.BlockSpec((1, D_MODEL, D_FF), lambda j, te, nv: (te[j], 0, 0)),
                pl.BlockSpec((1, D_FF, D_MODEL), lambda j, te, nv: (te[j], 0, 0)),
            ],
            out_specs=pl.BlockSpec((tm * ROW_SUB, LANES), lambda j, te, nv: (j, 0)),
        ),
        out_shape=jax.ShapeDtypeStruct((n_rows * ROW_SUB, LANES), F32),
        compiler_params=_params(1, n_arbitrary=1),
        name="l1_expert_ffn",
    )(tile_expert, n_valid, xs, wg, wu, wd)


def _combine_kernel(offs_ref, idx_ref, rank_ref, gates_ref, x_ref, ys_ref, g2_ref, gate_ref, out_ref,
                    ybuf, sem):
    tm = x_ref.shape[0]

    def issue(t, carry):
        for k in range(TOP_K):
            pos = offs_ref[idx_ref[k, t]] + rank_ref[k, t]
            pltpu.make_async_copy(_token_rows(ys_ref, pos), _token_rows(ybuf.at[k], t), sem).start()
        return carry

    lax.fori_loop(0, tm, issue, 0)
    pad = jnp.zeros((8 - TOP_K, tm), F32)
    gcols = jnp.concatenate([gates_ref[...], pad], axis=0).T
    for k in range(TOP_K):
        _token_copies_wait(ys_ref, ybuf.at[k], sem, tm)
    y = (gcols[:, 0:1] * _load_token_major(ybuf.at[0])
         + gcols[:, 1:2] * _load_token_major(ybuf.at[1]))
    out_ref[...] = x_ref[...] + gate_ref[...] * _rms(y, g2_ref[...])


def _combine(offs, idx, rank, gates, x2d, ys, mod_r, mod_base, g_post, seq):
    n = x2d.shape[0]
    tm = MOVE_TILE
    tiles_per_seq = seq // tm
    smem = functools.partial(pl.BlockSpec, memory_space=pltpu.SMEM)
    return pl.pallas_call(
        _combine_kernel,
        grid_spec=pltpu.PrefetchScalarGridSpec(
            num_scalar_prefetch=1,
            grid=(n // tm,),
            in_specs=[
                smem((TOP_K, tm), lambda i, offs: (0, i)),
                smem((TOP_K, tm), lambda i, offs: (0, i)),
                pl.BlockSpec((TOP_K, tm), lambda i, offs: (0, i)),
                pl.BlockSpec((tm, D_MODEL), lambda i, offs: (i, 0)),
                pl.BlockSpec(memory_space=pl.ANY),
                _resident((1, D_MODEL), lambda i, offs: (0, 0)),
                pl.BlockSpec((None, 1, D_MODEL),
                             lambda i, offs: (mod_base + (i // tiles_per_seq) * N_MOD + 5, 0, 0)),
            ],
            out_specs=pl.BlockSpec((tm, D_MODEL), lambda i, offs: (i, 0)),
            scratch_shapes=[
                pltpu.VMEM((TOP_K, tm * ROW_SUB, LANES), F32),
                pltpu.SemaphoreType.DMA(()),
            ],
        ),
        out_shape=jax.ShapeDtypeStruct((n, D_MODEL), F32),
        compiler_params=_params(1, n_arbitrary=1),
        name="l1_combine",
    )(offs, idx, rank, gates, x2d, ys, g_post, mod_r)


def _moe(x2d, mod_r, mod_base, g_pre, g_post, router_w, wg, wu, wd, seq):
    n = x2d.shape[0]
    hp, idx, gates, rank, counts = _route(x2d, mod_r, mod_base, g_pre, router_w, seq)
    n_tiles = TOP_K * n // EXPERT_TILE + N_EXPERTS
    cnt = counts[:, 0].astype(jnp.int32)
    tiles = (cnt + EXPERT_TILE - 1) // EXPERT_TILE
    tile_end = jnp.cumsum(tiles)
    offs = (tile_end - tiles) * EXPERT_TILE
    n_valid = tile_end[-1:]
    tile_ids = jnp.arange(n_tiles, dtype=jnp.int32)
    tile_expert = jnp.minimum(jnp.sum(tile_ids[:, None] >= tile_end[None, :], axis=1), N_EXPERTS - 1)
    tile_expert = jnp.where(tile_ids < n_valid[0], tile_expert, tile_expert[jnp.maximum(n_valid[0] - 1, 0)])
    xs = _dispatch(offs, idx, rank, hp, n_tiles * EXPERT_TILE)
    ys = _expert_ffn(tile_expert.astype(jnp.int32), n_valid, xs, wg, wu, wd)
    return _combine(offs, idx, rank, gates, x2d, ys, mod_r, mod_base, g_post, seq)


def kernel(x, c, mod_w, mod_b, norm_g, l0_w_in, l0_w_out, l0_ffn_w_gate, l0_ffn_w_up, l0_ffn_w_down,
           l1_w_in, l1_q_norm_g, l1_k_norm_g, l1_w_out, l1_router_w, l1_exp_w_gate, l1_exp_w_up,
           l1_exp_w_down):
    batch, seq, d = x.shape
    assert d == D_MODEL and seq % 2048 == 0
    n = batch * seq
    x2d = x.reshape(n, d)
    mod = _mod_vectors(c, mod_w, mod_b)
    mod_r = mod.reshape(mod.shape[0] * batch * N_MOD, 1, d)
    gains = norm_g.reshape(norm_g.shape[0], norm_g.shape[1], 1, d)

    base0 = 0
    gw = 3 * d
    o_list, lse_list = [], []
    for gi, (_, dil) in enumerate(DILATED_GROUPS):
        w_g = l0_w_in[:, gi * gw:(gi + 1) * gw].astype(BF16)
        qkv = _proj0(x2d, mod_r, base0, gains[0, 0], w_g, dil, batch, seq)
        o_g, lse_g = _attn0(qkv, dil)
        o_list.append(o_g)
        lse_list.append(lse_g)
    x2d = _merge_out(o_list, lse_list, x2d, mod_r, base0, l0_w_out.astype(BF16), gains[0, 1], batch, seq)
    x2d = _ffn(x2d, mod_r, base0, gains[0, 2], gains[0, 3], l0_ffn_w_gate.astype(BF16),
               l0_ffn_w_up.astype(BF16), l0_ffn_w_down.astype(BF16), seq)

    base1 = batch * N_MOD
    qt, kd, vt = _proj1(x2d, mod_r, base1, gains[1, 0], l1_w_in.astype(BF16), l1_q_norm_g, l1_k_norm_g,
                        batch, seq)
    o = _attn1(qt, kd, vt, batch, seq)
    x2d = _out_proj(o, x2d, mod_r, base1, l1_w_out.astype(BF16), gains[1, 1], seq)
    x2d = _moe(x2d, mod_r, base1, gains[1, 2], gains[1, 3], l1_router_w, l1_exp_w_gate.astype(BF16),
               l1_exp_w_up.astype(BF16), l1_exp_w_down.astype(BF16), seq)
    return x2d.reshape(batch, seq, d)
```

```python
import functools
import math

import numpy as np
import jax
import jax.numpy as jnp
from jax import lax
from jax.experimental import pallas as pl
from jax.experimental.pallas import tpu as pltpu

F32 = jnp.float32
BF16 = jnp.bfloat16

D_MODEL = 1024
N_HEADS = 16
HEAD_DIM = 64
LANES = 128
N_PAIRS = D_MODEL // LANES
DILATED_GROUPS = ((128, 1), (512, 4), (2048, 16))
N_SIDE = 64
N_KV_HEADS = 4
Q_PER_KV = N_HEADS // N_KV_HEADS
GRID_W = 64
ROPE_THETA = 10000.0
D_FF = 3584
N_EXPERTS = 8
N_MOD = 6
EPS = 1e-6
NEG_BIG = -1e30
VMEM_LIMIT_BYTES = 56 * 1024 * 1024


def _params(n_axes, n_arbitrary=0):
    sem = ("parallel",) * (n_axes - n_arbitrary) + ("arbitrary",) * n_arbitrary
    return pltpu.CompilerParams(dimension_semantics=sem, vmem_limit_bytes=VMEM_LIMIT_BYTES)


def _resident(shape, index_map):
    return pl.BlockSpec(shape, index_map, pipeline_mode=pl.Buffered(1))


def _rms(x, g):
    ms = jnp.mean(x * x, axis=-1, keepdims=True)
    return x * lax.rsqrt(ms + EPS) * g


def _prenorm(x, g, shift, scale):
    return _rms(x, g) * (1.0 + scale) + shift


def _sigmoid(x):
    return 1.0 / (1.0 + jnp.exp(-x))


def _mod_kernel(c_ref, w_ref, b_ref, o_ref):
    c = c_ref[...]
    ca = c * _sigmoid(c)
    o_ref[0] = jnp.dot(ca, w_ref[0], preferred_element_type=F32, precision=lax.Precision.HIGHEST) + b_ref[0]


def _mod_vectors(c, mod_w, mod_b):
    depth, d, n6 = mod_w.shape
    b = c.shape[0]
    tn = 1536
    return pl.pallas_call(
        _mod_kernel,
        grid=(depth, n6 // tn),
        in_specs=[
            pl.BlockSpec((b, d), lambda l, j: (0, 0)),
            pl.BlockSpec((1, d, tn), lambda l, j: (l, 0, j)),
            pl.BlockSpec((1, 1, tn), lambda l, j: (l, 0, j)),
        ],
        out_specs=pl.BlockSpec((1, b, tn), lambda l, j: (l, 0, j)),
        out_shape=jax.ShapeDtypeStruct((depth, b, n6), F32),
        compiler_params=_params(2),
        name="mod_vectors",
    )(c, mod_w, mod_b.reshape(depth, 1, n6))


PERM_SUB = 256


def _perm_matrix(d):
    p = np.zeros((PERM_SUB, PERM_SUB), np.float32)
    j = np.arange(PERM_SUB)
    p[(j % d) * (PERM_SUB // d) + j // d, j] = 1.0
    return p


def _proj0_kernel(x_ref, g_ref, sh_ref, sc_ref, p_ref, w_ref, o_ref, h_scr, *, d, tm, cw):
    h = _prenorm(x_ref[...], g_ref[...], sh_ref[...], sc_ref[...]).astype(BF16)
    per = tm // d
    if d == 1:
        h_scr[...] = h
    else:
        sub_per = PERM_SUB // d
        for s in range(tm // PERM_SUB):
            hs = jnp.dot(p_ref[...], h[s * PERM_SUB:(s + 1) * PERM_SUB], preferred_element_type=F32).astype(BF16)
            for r in range(d):
                h_scr[r * per + s * sub_per:r * per + (s + 1) * sub_per, :] = hs[r * sub_per:(r + 1) * sub_per]
    hp = h_scr[...]
    for j in range(w_ref.shape[1] // cw):
        res = jnp.dot(hp, w_ref[:, j * cw:(j + 1) * cw], preferred_element_type=F32).astype(BF16)
        for r in range(d):
            o_ref[r, :, j * cw:(j + 1) * cw] = res[r * per:(r + 1) * per]


def _proj0(x2d, mod_r, mod_base, g, w, d, batch, seq):
    n = x2d.shape[0]
    tm = 512
    tiles_per_seq = seq // tm
    ncol = w.shape[1]
    kern = functools.partial(_proj0_kernel, d=d, tm=tm, cw=512)
    perm = jnp.asarray(_perm_matrix(d), BF16)
    return pl.pallas_call(
        kern,
        grid=(n // tm,),
        in_specs=[
            pl.BlockSpec((tm, D_MODEL), lambda i: (i, 0)),
            _resident((1, D_MODEL), lambda i: (0, 0)),
            pl.BlockSpec((None, 1, D_MODEL), lambda i: (mod_base + (i // tiles_per_seq) * N_MOD + 0, 0, 0)),
            pl.BlockSpec((None, 1, D_MODEL), lambda i: (mod_base + (i // tiles_per_seq) * N_MOD + 1, 0, 0)),
            _resident((PERM_SUB, PERM_SUB), lambda i: (0, 0)),
            _resident((D_MODEL, ncol), lambda i: (0, 0)),
        ],
        out_specs=pl.BlockSpec((None, d, tm // d, ncol), lambda i: (i // tiles_per_seq, 0, i % tiles_per_seq, 0)),
        out_shape=jax.ShapeDtypeStruct((batch, d, seq // d, ncol), BF16),
        scratch_shapes=[pltpu.VMEM((tm, D_MODEL), BF16)],
        compiler_params=_params(1),
        name=f"l0_qkv_d{d}",
    )(x2d, g, mod_r, mod_r, perm, w)


ATT0_SQ = 128
ATT0_TK = ATT0_SQ + 2 * N_SIDE


def _attn0_kernel(q_ref, kp_ref, kc_ref, kn_ref, vp_ref, vc_ref, vn_ref, o_ref, lse_ref, kbuf, vbuf,
                  *, d, tq, length):
    i = pl.program_id(2)
    kbuf[0:N_SIDE] = kp_ref[...]
    kbuf[N_SIDE:N_SIDE + tq] = kc_ref[...]
    kbuf[N_SIDE + tq:] = kn_ref[...]
    vbuf[0:N_SIDE] = vp_ref[...]
    vbuf[N_SIDE:N_SIDE + tq] = vc_ref[...]
    vbuf[N_SIDE + tq:] = vn_ref[...]

    lane = lax.broadcasted_iota(jnp.int32, (ATT0_SQ, LANES), 1)
    low = lane < HEAD_DIM
    a_idx = lax.broadcasted_iota(jnp.int32, (ATT0_SQ, ATT0_TK), 0)
    c_idx = lax.broadcasted_iota(jnp.int32, (ATT0_SQ, ATT0_TK), 1)
    absrel = jnp.abs(c_idx - N_SIDE - a_idx)

    def scores(j, p):
        cols = slice(p * LANES, (p + 1) * LANES)
        q2 = q_ref[j * ATT0_SQ:(j + 1) * ATT0_SQ, cols] * (HEAD_DIM ** -0.5 * LOG2_E)
        zero = jnp.zeros_like(q2)
        qs = jnp.concatenate([jnp.where(low, q2, zero), jnp.where(low, zero, q2)], axis=0)
        k2 = kbuf[j * ATT0_SQ:j * ATT0_SQ + ATT0_TK, cols]
        return lax.dot_general(qs, k2, (((1,), (1,)), ((), ())), preferred_element_type=F32)

    items = [(j, p) for j in range(tq // ATT0_SQ) for p in range(N_PAIRS)]
    s_next = scores(*items[0])
    for n, (j, p) in enumerate(items):
        s = s_next
        if n + 1 < len(items):
            s_next = scores(*items[n + 1])
        if p == 0:
            u_key = i * tq + (j * ATT0_SQ - N_SIDE) + c_idx
            valid = (absrel <= N_SIDE) & (u_key >= 0) & (u_key < length)
            base = jnp.where(valid, (-float(d)) * absrel.astype(F32), NEG_BIG)
            lse_tile = jnp.zeros((ATT0_SQ, LANES), F32)
        cols = slice(p * LANES, (p + 1) * LANES)
        v2 = vbuf[j * ATT0_SQ:j * ATT0_SQ + ATT0_TK, cols]
        ps, ls = [], []
        for hh in range(2):
            head = 2 * p + hh
            slope = 2.0 ** (-8.0 * (head + 1) / N_HEADS)
            sh = s[hh * ATT0_SQ:(hh + 1) * ATT0_SQ] + (slope * LOG2_E) * base
            m = jnp.max(sh, axis=-1, keepdims=True)
            e = jnp.exp2(sh - m)
            l = jnp.sum(e, axis=-1, keepdims=True)
            ps.append(e.astype(BF16))
            ls.append(l)
            lse_tile = jnp.where(lane == head, m * (1.0 / LOG2_E) + jnp.log(l), lse_tile)
        o = jnp.dot(jnp.concatenate(ps, axis=0), v2, preferred_element_type=F32)
        o2 = jnp.where(low, o[:ATT0_SQ] / ls[0], o[ATT0_SQ:] / ls[1])
        o_ref[j * ATT0_SQ:(j + 1) * ATT0_SQ, cols] = o2.astype(BF16)
        if p == N_PAIRS - 1:
            lse_ref[j * ATT0_SQ:(j + 1) * ATT0_SQ, :] = lse_tile


def _attn0(qkv, d):
    batch, _, length, _ = qkv.shape
    tq = min(512, length)
    nb = tq // N_SIDE
    last = length // N_SIDE - 1
    kern = functools.partial(_attn0_kernel, d=d, tq=tq, length=length)

    def main(col):
        return pl.BlockSpec((None, None, tq, D_MODEL), lambda b, r, i: (b, r, i, col))

    def prev(col):
        return pl.BlockSpec((None, None, N_SIDE, D_MODEL), lambda b, r, i: (b, r, jnp.maximum(i * nb - 1, 0), col))

    def nxt(col):
        return pl.BlockSpec((None, None, N_SIDE, D_MODEL), lambda b, r, i: (b, r, jnp.minimum((i + 1) * nb, last), col))

    return pl.pallas_call(
        kern,
        grid=(batch, d, length // tq),
        in_specs=[main(0), prev(1), main(1), nxt(1), prev(2), main(2), nxt(2)],
        out_specs=[
            pl.BlockSpec((None, None, tq, D_MODEL), lambda b, r, i: (b, r, i, 0)),
            pl.BlockSpec((None, None, tq, LANES), lambda b, r, i: (b, r, i, 0)),
        ],
        out_shape=[
            jax.ShapeDtypeStruct((batch, d, length, D_MODEL), BF16),
            jax.ShapeDtypeStruct((batch, d, length, LANES), F32),
        ],
        scratch_shapes=[
            pltpu.VMEM((tq + 2 * N_SIDE, D_MODEL), BF16),
            pltpu.VMEM((tq + 2 * N_SIDE, D_MODEL), BF16),
        ],
        compiler_params=_params(3),
        name=f"l0_attn_d{d}",
    )(qkv, qkv, qkv, qkv, qkv, qkv, qkv)


def _split2(x):
    hi = x.astype(BF16)
    lo = (x - hi.astype(F32)).astype(BF16)
    return hi, lo


def _unpermute(pt, blk_ref, d, tm, exact_f32):
    sub_per = PERM_SUB // d
    outs = []
    for s in range(tm // PERM_SUB):
        src = jnp.concatenate([blk_ref[r, s * sub_per:(s + 1) * sub_per, :] for r in range(d)], axis=0)
        if exact_f32:
            nat = sum(jnp.dot(pt, part, preferred_element_type=F32) for part in _split2(src))
        else:
            nat = jnp.dot(pt, src, preferred_element_type=F32)
        outs.append(nat)
    return jnp.concatenate(outs, axis=0)


def _merge_out_kernel(o0_ref, o1_ref, o2_ref, l0_ref, l1_ref, l2_ref, p1_ref, p2_ref, e_ref,
                      x_ref, w_ref, g_ref, gate_ref, out_ref, *, tm):
    dils = [dil for _, dil in DILATED_GROUPS]
    o_refs = (o0_ref, o1_ref, o2_ref)
    l_refs = (l0_ref, l1_ref, l2_ref)
    pts = (None, p1_ref[...], p2_ref[...])
    os_, ls_ = [], []
    for gi, d in enumerate(dils):
        if d == 1:
            os_.append(o_refs[gi][0].astype(F32))
            ls_.append(l_refs[gi][0])
        else:
            os_.append(_unpermute(pts[gi], o_refs[gi], d, tm, False))
            ls_.append(_unpermute(pts[gi], l_refs[gi], d, tm, True))
    m = jnp.maximum(jnp.maximum(ls_[0], ls_[1]), ls_[2])
    es = [jnp.exp(l - m) for l in ls_]
    den = es[0] + es[1] + es[2]
    acc = jnp.zeros((tm, D_MODEL), F32)
    for gi in range(3):
        wgt = es[gi] / den
        hi, lo = _split2(wgt)
        wfull = (jnp.dot(hi, e_ref[...], preferred_element_type=F32)
                 + jnp.dot(lo, e_ref[...], preferred_element_type=F32))
        acc = acc + wfull * os_[gi]
    y = jnp.dot(acc.astype(BF16), w_ref[...], preferred_element_type=F32)
    out_ref[...] = x_ref[...] + gate_ref[...] * _rms(y, g_ref[...])


def _merge_out(o_list, lse_list, x2d, mod_r, mod_base, w_out, g, batch, seq):
    n = x2d.shape[0]
    tm = 512
    tiles_per_seq = seq // tm
    kern = functools.partial(_merge_out_kernel, tm=tm)
    dils = [dil for _, dil in DILATED_GROUPS]
    expand = np.zeros((LANES, D_MODEL), np.float32)
    for h in range(N_HEADS):
        expand[h, h * HEAD_DIM:(h + 1) * HEAD_DIM] = 1.0

    def grouped(d, w):
        return pl.BlockSpec((None, d, tm // d, w), lambda i: (i // tiles_per_seq, 0, i % tiles_per_seq, 0))

    in_specs = ([grouped(d, D_MODEL) for d in dils] + [grouped(d, LANES) for d in dils] + [
        _resident((PERM_SUB, PERM_SUB), lambda i: (0, 0)),
        _resident((PERM_SUB, PERM_SUB), lambda i: (0, 0)),
        _resident((LANES, D_MODEL), lambda i: (0, 0)),
        pl.BlockSpec((tm, D_MODEL), lambda i: (i, 0)),
        _resident((D_MODEL, D_MODEL), lambda i: (0, 0)),
        _resident((1, D_MODEL), lambda i: (0, 0)),
        pl.BlockSpec((None, 1, D_MODEL), lambda i: (mod_base + (i // tiles_per_seq) * N_MOD + 2, 0, 0)),
    ])
    return pl.pallas_call(
        kern,
        grid=(n // tm,),
        in_specs=in_specs,
        out_specs=pl.BlockSpec((tm, D_MODEL), lambda i: (i, 0)),
        out_shape=jax.ShapeDtypeStruct((n, D_MODEL), F32),
        compiler_params=_params(1),
        name="l0_merge_out",
    )(*o_list, *lse_list,
      jnp.asarray(_perm_matrix(dils[1]).T, BF16), jnp.asarray(_perm_matrix(dils[2]).T, BF16),
      jnp.asarray(expand, BF16), x2d, w_out, g, mod_r)


def _out_kernel(o_ref, x_ref, w_ref, g_ref, gate_ref, out_ref):
    y = jnp.dot(o_ref[...], w_ref[...], preferred_element_type=F32)
    out_ref[...] = x_ref[...] + gate_ref[...] * _rms(y, g_ref[...])


def _out_proj(o2d, x2d, mod_r, mod_base, w_out, g, seq):
    n = x2d.shape[0]
    tm = 512
    tiles_per_seq = seq // tm
    return pl.pallas_call(
        _out_kernel,
        grid=(n // tm,),
        in_specs=[
            pl.BlockSpec((tm, D_MODEL), lambda i: (i, 0)),
            pl.BlockSpec((tm, D_MODEL), lambda i: (i, 0)),
            _resident((D_MODEL, D_MODEL), lambda i: (0, 0)),
            _resident((1, D_MODEL), lambda i: (0, 0)),
            pl.BlockSpec((None, 1, D_MODEL), lambda i: (mod_base + (i // tiles_per_seq) * N_MOD + 2, 0, 0)),
        ],
        out_specs=pl.BlockSpec((tm, D_MODEL), lambda i: (i, 0)),
        out_shape=jax.ShapeDtypeStruct((n, D_MODEL), F32),
        compiler_params=_params(1),
        name="l1_out_proj",
    )(o2d, x2d, w_out, g, mod_r)


FF_CHUNK = 512


def _swiglu_acc(h, wg_ref, wu_ref, wd_ref, acc):
    for c in range(wg_ref.shape[-1] // FF_CHUNK):
        cols = slice(c * FF_CHUNK, (c + 1) * FF_CHUNK)
        a = jnp.dot(h, wg_ref[:, cols], preferred_element_type=F32)
        u = jnp.dot(h, wu_ref[:, cols], preferred_element_type=F32)
        t = (a * _sigmoid(a) * u).astype(BF16)
        acc = acc + jnp.dot(t, wd_ref[cols, :], preferred_element_type=F32)
    return acc


def _ffn_kernel(x_ref, g_ref, sh_ref, sc_ref, wg_ref, wu_ref, wd_ref, g2_ref, gate_ref, out_ref):
    x = x_ref[...]
    h = _prenorm(x, g_ref[...], sh_ref[...], sc_ref[...]).astype(BF16)
    y = _swiglu_acc(h, wg_ref, wu_ref, wd_ref, jnp.zeros(x.shape, F32))
    out_ref[...] = x + gate_ref[...] * _rms(y, g2_ref[...])


def _ffn(x2d, mod_r, mod_base, g_pre, g_post, wg, wu, wd, seq):
    n = x2d.shape[0]
    tm = 512
    tiles_per_seq = seq // tm

    def modspec(k):
        return pl.BlockSpec((None, 1, D_MODEL), lambda i: (mod_base + (i // tiles_per_seq) * N_MOD + k, 0, 0))

    return pl.pallas_call(
        _ffn_kernel,
        grid=(n // tm,),
        in_specs=[
            pl.BlockSpec((tm, D_MODEL), lambda i: (i, 0)),
            _resident((1, D_MODEL), lambda i: (0, 0)),
            modspec(3), modspec(4),
            _resident((D_MODEL, D_FF), lambda i: (0, 0)),
            _resident((D_MODEL, D_FF), lambda i: (0, 0)),
            _resident((D_FF, D_MODEL), lambda i: (0, 0)),
            _resident((1, D_MODEL), lambda i: (0, 0)),
            modspec(5),
        ],
        out_specs=pl.BlockSpec((tm, D_MODEL), lambda i: (i, 0)),
        out_shape=jax.ShapeDtypeStruct((n, D_MODEL), F32),
        compiler_params=_params(1),
        name="l0_ffn",
    )(x2d, g_pre, mod_r, mod_r, wg, wu, wd, g_post, mod_r)


ATT1_TK = 512
VT_ROWS = HEAD_DIM + 16
LOG2_E = math.log2(math.e)


def _proj1_kernel(x_ref, g_ref, sh_ref, sc_ref, w_ref, m2_ref, qg_ref, kg_ref, cos_ref, sn_ref, sp_ref,
                  qt_ref, kd_ref, vt_ref):
    h = _prenorm(x_ref[...], g_ref[...], sh_ref[...], sc_ref[...]).astype(BF16)
    proj = jnp.dot(h, w_ref[...], preferred_element_type=F32)
    tm = proj.shape[0]
    lane = lax.broadcasted_iota(jnp.int32, (tm, LANES), 1)
    low = lane < HEAD_DIM
    top = lax.broadcasted_iota(jnp.int32, (LANES, tm), 0) < HEAD_DIM
    cos, sn, sp = cos_ref[...], sn_ref[...], sp_ref[...]
    n_q = D_MODEL // LANES
    n_k = N_KV_HEADS * HEAD_DIM // LANES
    for c in range(n_q + n_k):
        z = proj[:, c * LANES:(c + 1) * LANES]
        zz = z * z
        hi = zz.astype(BF16)
        lo = (zz - hi.astype(F32)).astype(BF16)
        ms = jnp.dot(hi, m2_ref[...], preferred_element_type=F32) + jnp.dot(lo, m2_ref[...], preferred_element_type=F32)
        gain = qg_ref[...] if c < n_q else kg_ref[...]
        zn = z * lax.rsqrt(ms + EPS) * gain
        zr = zn * cos + pltpu.roll(zn, LANES - 16, 1) * sn + pltpu.roll(zn, 16, 1) * sp
        if c < n_q:
            zt = (zr * (HEAD_DIM ** -0.5 * LOG2_E)).T.astype(BF16)
            zero = jnp.zeros_like(zt)
            qt_ref[2 * c, 0] = jnp.where(top, zt, zero)
            qt_ref[2 * c + 1, 0] = jnp.where(top, zero, zt)
        else:
            j = c - n_q
            sw = pltpu.roll(zr, HEAD_DIM, 1)
            kd_ref[:, (2 * j) * LANES:(2 * j + 1) * LANES] = jnp.where(low, zr, sw).astype(BF16)
            kd_ref[:, (2 * j + 1) * LANES:(2 * j + 2) * LANES] = jnp.where(low, sw, zr).astype(BF16)
    v_base = D_MODEL + N_KV_HEADS * HEAD_DIM
    ones = jnp.ones((VT_ROWS - HEAD_DIM, tm), BF16)
    for j in range(n_k):
        zt = proj[:, v_base + j * LANES:v_base + (j + 1) * LANES].T
        for hh in range(2):
            vt_ref[2 * j + hh, 0, 0:HEAD_DIM, :] = zt[hh * HEAD_DIM:(hh + 1) * HEAD_DIM].astype(BF16)
            vt_ref[2 * j + hh, 0, HEAD_DIM:, :] = ones


def _rope_tables(seq):
    lane = np.arange(LANES)
    dd = lane % HEAD_DIM
    blk = dd // (HEAD_DIM // 2)
    idx = dd % (HEAD_DIM // 2)
    half = HEAD_DIM // 4
    fi = idx % half
    first = idx < half
    freqs = jnp.asarray(ROPE_THETA, F32) ** (-jnp.arange(half, dtype=F32) / half)
    t = jnp.arange(seq)
    row = (t // GRID_W).astype(F32)
    col = (t % GRID_W).astype(F32)
    pos = jnp.where(jnp.asarray(blk == 0)[None, :], row[:, None], col[:, None])
    ang = pos * freqs[jnp.asarray(fi)][None, :]
    cos = jnp.cos(ang)
    sin = jnp.sin(ang)
    first = jnp.asarray(first)[None, :]
    return cos, jnp.where(first, -sin, 0.0), jnp.where(first, 0.0, sin)


def _proj1(x2d, mod_r, mod_base, g, w, qg, kg, batch, seq):
    n = x2d.shape[0]
    tm = ATT1_TK
    tiles_per_seq = seq // tm
    ncol = w.shape[1]
    kvw = N_KV_HEADS * LANES
    m2 = np.zeros((LANES, LANES), np.float32)
    m2[:HEAD_DIM, :HEAD_DIM] = 1.0 / HEAD_DIM
    m2[HEAD_DIM:, HEAD_DIM:] = 1.0 / HEAD_DIM
    cos, sn, sp = _rope_tables(seq)
    qg2 = jnp.tile(qg.astype(F32), 2).reshape(1, LANES)
    kg2 = jnp.tile(kg.astype(F32), 2).reshape(1, LANES)

    def modspec(k):
        return pl.BlockSpec((None, 1, D_MODEL), lambda i: (mod_base + (i // tiles_per_seq) * N_MOD + k, 0, 0))

    def table():
        return pl.BlockSpec((tm, LANES), lambda i: (i % tiles_per_seq, 0))

    return pl.pallas_call(
        _proj1_kernel,
        grid=(n // tm,),
        in_specs=[
            pl.BlockSpec((tm, D_MODEL), lambda i: (i, 0)),
            _resident((1, D_MODEL), lambda i: (0, 0)),
            modspec(0), modspec(1),
            _resident((D_MODEL, ncol), lambda i: (0, 0)),
            _resident((LANES, LANES), lambda i: (0, 0)),
            _resident((1, LANES), lambda i: (0, 0)),
            _resident((1, LANES), lambda i: (0, 0)),
            table(), table(), table(),
        ],
        out_specs=[
            pl.BlockSpec((None, N_HEADS, 1, LANES, tm), lambda i: (i // tiles_per_seq, 0, i % tiles_per_seq, 0, 0)),
            pl.BlockSpec((tm, kvw), lambda i: (i, 0)),
            pl.BlockSpec((None, N_KV_HEADS, 1, VT_ROWS, tm),
                         lambda i: (i // tiles_per_seq, 0, i % tiles_per_seq, 0, 0)),
        ],
        out_shape=[
            jax.ShapeDtypeStruct((batch, N_HEADS, tiles_per_seq, LANES, tm), BF16),
            jax.ShapeDtypeStruct((n, kvw), BF16),
            jax.ShapeDtypeStruct((batch, N_KV_HEADS, tiles_per_seq, VT_ROWS, tm), BF16),
        ],
        compiler_params=_params(1),
        name="l1_qkv_rope",
    )(x2d, g, mod_r, mod_r, w, jnp.asarray(m2, BF16), qg2, kg2, cos, sn, sp)


def _attn1_kernel(qt_ref, k_ref, vt_ref, o_ref, st_scr, cmax_scr, m_scr, acc_scr, *, tq, tk, seq):
    nk = seq // tk
    nq = seq // tq
    total = nq * nk

    def reset_state():
        m_scr[...] = jnp.full(m_scr.shape, NEG_BIG, F32)
        acc_scr[...] = jnp.zeros(acc_scr.shape, F32)

    def scores(h, qi, kt, buf):
        start = pl.multiple_of(kt * tk, tk)
        st = jnp.dot(k_ref[pl.ds(start, tk), :], qt_ref[h, qi], preferred_element_type=F32)
        st_scr[buf, h] = st
        cmax_scr[buf, h] = jnp.max(st, axis=0, keepdims=True)

    def softmax_pv(h, kt, buf):
        st = st_scr[buf, h]
        m_prev = m_scr[h]
        m_new = jnp.maximum(m_prev, cmax_scr[buf, h])
        p = jnp.exp2(st - m_new).astype(BF16)
        acc_scr[h] = jnp.exp2(m_prev - m_new) * acc_scr[h] + jnp.dot(vt_ref[kt], p, preferred_element_type=F32)
        m_scr[h] = m_new

    def step(kt, qi_next, kt_next, cur, nxt):
        scores(0, qi_next, kt_next, nxt)
        scores(1, qi_next, kt_next, nxt)
        softmax_pv(0, kt, cur)
        scores(2, qi_next, kt_next, nxt)
        softmax_pv(1, kt, cur)
        scores(3, qi_next, kt_next, nxt)
        softmax_pv(2, kt, cur)
        softmax_pv(3, kt, cur)

    def finish_tile(qi):
        start = pl.multiple_of(qi * tq, tq)
        for g2 in range(Q_PER_KV // 2):
            parts = []
            for hh in range(2):
                a = acc_scr[2 * g2 + hh]
                parts.append(a[:HEAD_DIM] / a[HEAD_DIM:HEAD_DIM + 1])
            o_ref[pl.ds(start, tq), g2 * LANES:(g2 + 1) * LANES] = jnp.concatenate(parts, axis=0).T.astype(BF16)
        reset_state()

    reset_state()
    for h in range(Q_PER_KV):
        scores(h, 0, 0, 0)

    def tile_body(qi, carry):
        def pair_body(i, inner):
            kt = 2 * i
            step(kt, qi, kt + 1, 0, 1)
            c2 = jnp.minimum(qi * nk + kt + 2, total - 1)
            step(kt + 1, c2 // nk, c2 % nk, 1, 0)
            return inner

        lax.fori_loop(0, nk // 2, pair_body, 0)
        finish_tile(qi)
        return carry

    lax.fori_loop(0, nq, tile_body, 0)


def _attn1(qt, kd, vt, batch, seq):
    tq, tk = ATT1_TK, ATT1_TK
    qw = Q_PER_KV * HEAD_DIM
    assert (seq // tk) % 2 == 0
    kern = functools.partial(_attn1_kernel, tq=tq, tk=tk, seq=seq)
    return pl.pallas_call(
        kern,
        grid=(batch, N_KV_HEADS),
        in_specs=[
            pl.BlockSpec((None, Q_PER_KV, seq // tq, LANES, tq), lambda b, j: (b, j, 0, 0, 0)),
            pl.BlockSpec((seq, LANES), lambda b, j: (b, j)),
            pl.BlockSpec((None, None, seq // tk, VT_ROWS, tk), lambda b, j: (b, j, 0, 0, 0)),
        ],
        out_specs=pl.BlockSpec((seq, qw), lambda b, j: (b, j)),
        out_shape=jax.ShapeDtypeStruct((batch * seq, D_MODEL), BF16),
        scratch_shapes=[
            pltpu.VMEM((2, Q_PER_KV, tk, tq), F32),
            pltpu.VMEM((2, Q_PER_KV, 1, tq), F32),
            pltpu.VMEM((Q_PER_KV, 1, tq), F32),
            pltpu.VMEM((Q_PER_KV, VT_ROWS, tq), F32),
        ],
        compiler_params=_params(2),
        name="l1_attn",
    )(qt, kd, vt)


TOP_K = 2
ROW_SUB = D_MODEL // LANES
EXPERT_TILE = 512
ROUTE_TILE = 512
MOVE_TILE = 512


def _store_token_major(ref, val):
    tm = val.shape[0]
    for s in range(ROW_SUB):
        ref[pl.ds(s, tm, stride=ROW_SUB), :] = val[:, s * LANES:(s + 1) * LANES]


def _load_token_major(ref):
    tm = ref.shape[0] // ROW_SUB
    return jnp.concatenate([ref[pl.ds(s, tm, stride=ROW_SUB), :] for s in range(ROW_SUB)], axis=-1)


def _token_rows(ref, t):
    return ref.at[pl.ds(pl.multiple_of(t * ROW_SUB, ROW_SUB), ROW_SUB)]


def _route_kernel(x_ref, g_ref, sh_ref, sc_ref, rw_ref, tri_ref, hp_ref, idx_ref, gates_ref, rank_ref,
                  counts_ref, run_scr):
    i = pl.program_id(0)
    tm = x_ref.shape[0]
    h = _prenorm(x_ref[...], g_ref[...], sh_ref[...], sc_ref[...])
    _store_token_major(hp_ref, h)
    logits = jnp.dot(h, rw_ref[...], preferred_element_type=F32, precision=lax.Precision.HIGHEST)
    lt = logits.T[:N_EXPERTS]
    sub = lax.broadcasted_iota(jnp.int32, (N_EXPERTS, tm), 0)
    m1 = jnp.max(lt, axis=0, keepdims=True)
    i1 = jnp.min(jnp.where(lt == m1, sub, N_EXPERTS), axis=0, keepdims=True)
    rest = jnp.where(sub == i1, NEG_BIG, lt)
    m2 = jnp.max(rest, axis=0, keepdims=True)
    i2 = jnp.min(jnp.where(rest == m2, sub, N_EXPERTS), axis=0, keepdims=True)
    t = jnp.exp(m2 - m1)
    gates_ref[0:1, :] = 1.0 / (1.0 + t)
    gates_ref[1:2, :] = t / (1.0 + t)
    idx_ref[0:1, :] = i1
    idx_ref[1:2, :] = i2

    @pl.when(i == 0)
    def _init():
        run_scr[...] = jnp.zeros(run_scr.shape, F32)

    onehot = jnp.where((sub == i1) | (sub == i2), 1.0, 0.0)
    before = jnp.dot(onehot.astype(BF16), tri_ref[...], preferred_element_type=F32) + run_scr[:, 0:1]
    rank_ref[0:1, :] = jnp.sum(jnp.where(sub == i1, before, 0.0), axis=0, keepdims=True).astype(jnp.int32)
    rank_ref[1:2, :] = jnp.sum(jnp.where(sub == i2, before, 0.0), axis=0, keepdims=True).astype(jnp.int32)
    run_scr[...] = run_scr[...] + jnp.sum(onehot, axis=1, keepdims=True)
    counts_ref[...] = run_scr[...]


def _route(x2d, mod_r, mod_base, g_pre, router_w, seq):
    n = x2d.shape[0]
    tm = ROUTE_TILE
    tiles_per_seq = seq // tm
    rw = jnp.zeros((D_MODEL, LANES), F32).at[:, :N_EXPERTS].set(router_w.astype(F32))
    tri = jnp.asarray(np.triu(np.ones((tm, tm), np.float32), k=1), BF16)

    def modspec(k):
        return pl.BlockSpec((None, 1, D_MODEL), lambda i: (mod_base + (i // tiles_per_seq) * N_MOD + k, 0, 0))

    def per_token():
        return pl.BlockSpec((TOP_K, tm), lambda i: (0, i))

    return pl.pallas_call(
        _route_kernel,
        grid=(n // tm,),
        in_specs=[
            pl.BlockSpec((tm, D_MODEL), lambda i: (i, 0)),
            _resident((1, D_MODEL), lambda i: (0, 0)),
            modspec(3), modspec(4),
            _resident((D_MODEL, LANES), lambda i: (0, 0)),
            _resident((tm, tm), lambda i: (0, 0)),
        ],
        out_specs=[
            pl.BlockSpec((tm * ROW_SUB, LANES), lambda i: (i, 0)),
            per_token(), per_token(), per_token(),
            pl.BlockSpec((N_EXPERTS, LANES), lambda i: (0, 0)),
        ],
        out_shape=[
            jax.ShapeDtypeStruct((n * ROW_SUB, LANES), F32),
            jax.ShapeDtypeStruct((TOP_K, n), jnp.int32),
            jax.ShapeDtypeStruct((TOP_K, n), F32),
            jax.ShapeDtypeStruct((TOP_K, n), jnp.int32),
            jax.ShapeDtypeStruct((N_EXPERTS, LANES), F32),
        ],
        scratch_shapes=[pltpu.VMEM((N_EXPERTS, LANES), F32)],
        compiler_params=_params(1, n_arbitrary=1),
        name="l1_route",
    )(x2d, g_pre, mod_r, mod_r, rw, tri)


def _token_copies_wait(src_ref, dst_ref, sem, tokens):
    rows = tokens * ROW_SUB
    pltpu.make_async_copy(src_ref.at[pl.ds(0, rows)], dst_ref.at[pl.ds(0, rows)], sem).wait()


def _dispatch_kernel(offs_ref, idx_ref, rank_ref, hp_ref, zeros_ref, xs_ref, sem):
    del zeros_ref
    tm = hp_ref.shape[0] // ROW_SUB

    def issue(t, carry):
        for k in range(TOP_K):
            pos = offs_ref[idx_ref[k, t]] + rank_ref[k, t]
            pltpu.make_async_copy(_token_rows(hp_ref, t), _token_rows(xs_ref, pos), sem).start(priority=k % 2)
        return carry

    lax.fori_loop(0, tm, issue, 0)
    for k in range(TOP_K):
        _token_copies_wait(hp_ref, xs_ref, sem, tm)


def _dispatch(offs, idx, rank, hp, n_rows):
    n = hp.shape[0] // ROW_SUB
    tm = MOVE_TILE
    zeros = jnp.zeros((n_rows * ROW_SUB, LANES), F32)
    smem = functools.partial(pl.BlockSpec, memory_space=pltpu.SMEM)
    return pl.pallas_call(
        _dispatch_kernel,
        grid_spec=pltpu.PrefetchScalarGridSpec(
            num_scalar_prefetch=1,
            grid=(n // tm,),
            in_specs=[
                smem((TOP_K, tm), lambda i, offs: (0, i)),
                smem((TOP_K, tm), lambda i, offs: (0, i)),
                pl.BlockSpec((tm * ROW_SUB, LANES), lambda i, offs: (i, 0)),
                pl.BlockSpec(memory_space=pl.ANY),
            ],
            out_specs=pl.BlockSpec(memory_space=pl.ANY),
            scratch_shapes=[pltpu.SemaphoreType.DMA(())],
        ),
        out_shape=jax.ShapeDtypeStruct((n_rows * ROW_SUB, LANES), F32),
        input_output_aliases={4: 0},
        compiler_params=_params(1, n_arbitrary=1),
        name="l1_dispatch",
    )(offs, idx, rank, hp, zeros)


def _expert_ffn_kernel(tile_expert_ref, n_valid_ref, xs_ref, wg_ref, wu_ref, wd_ref, ys_ref):
    j = pl.program_id(0)

    @pl.when(j < n_valid_ref[0])
    def _compute():
        h = _load_token_major(xs_ref).astype(BF16)
        y = _swiglu_acc(h, wg_ref.at[0], wu_ref.at[0], wd_ref.at[0], jnp.zeros((h.shape[0], D_MODEL), F32))
        _store_token_major(ys_ref, y)

    @pl.when(j >= n_valid_ref[0])
    def _skip():
        ys_ref[...] = jnp.zeros(ys_ref.shape, F32)


def _expert_ffn(tile_expert, n_valid, xs, wg, wu, wd):
    n_rows = xs.shape[0] // ROW_SUB
    tm = EXPERT_TILE

    def row_tile(j, te, nv):
        return (jnp.minimum(j, nv[0] - 1), 0)

    return pl.pallas_call(
        _expert_ffn_kernel,
        grid_spec=pltpu.PrefetchScalarGridSpec(
            num_scalar_prefetch=2,
            grid=(n_rows // tm,),
            in_specs=[
                pl.BlockSpec((tm * ROW_SUB, LANES), row_tile),
                pl.BlockSpec((1, D_MODEL, D_FF), lambda j, te, nv: (te[j], 0, 0)),
                pl.BlockSpec((1, D_MODEL, D_FF), lambda j, te, nv: (te[j], 0, 0)),
                pl.BlockSpec((1, D_FF, D_MODEL), lambda j, te, nv: (te[j], 0, 0)),
            ],
            out_specs=pl.BlockSpec((tm * ROW_SUB, LANES), lambda j, te, nv: (j, 0)),
        ),
        out_shape=jax.ShapeDtypeStruct((n_rows * ROW_SUB, LANES), F32),
        compiler_params=_params(1, n_arbitrary=1),
        name="l1_expert_ffn",
    )(tile_expert, n_valid, xs, wg, wu, wd)


def _combine_kernel(offs_ref, idx_ref, rank_ref, gates_ref, x_ref, ys_ref, g2_ref, gate_ref, out_ref,
                    ybuf, sem):
    tm = x_ref.shape[0]

    def issue(t, carry):
        for k in range(TOP_K):
            pos = offs_ref[idx_ref[k, t]] + rank_ref[k, t]
            pltpu.make_async_copy(_token_rows(ys_ref, pos), _token_rows(ybuf.at[k], t), sem).start(priority=k % 2)
        return carry

    lax.fori_loop(0, tm, issue, 0)
    pad = jnp.zeros((8 - TOP_K, tm), F32)
    gcols = jnp.concatenate([gates_ref[...], pad], axis=0).T
    for k in range(TOP_K):
        _token_copies_wait(ys_ref, ybuf.at[k], sem, tm)
    y = (gcols[:, 0:1] * _load_token_major(ybuf.at[0])
         + gcols[:, 1:2] * _load_token_major(ybuf.at[1]))
    out_ref[...] = x_ref[...] + gate_ref[...] * _rms(y, g2_ref[...])


def _combine(offs, idx, rank, gates, x2d, ys, mod_r, mod_base, g_post, seq):
    n = x2d.shape[0]
    tm = MOVE_TILE
    tiles_per_seq = seq // tm
    smem = functools.partial(pl.BlockSpec, memory_space=pltpu.SMEM)
    return pl.pallas_call(
        _combine_kernel,
        grid_spec=pltpu.PrefetchScalarGridSpec(
            num_scalar_prefetch=1,
            grid=(n // tm,),
            in_specs=[
                smem((TOP_K, tm), lambda i, offs: (0, i)),
                smem((TOP_K, tm), lambda i, offs: (0, i)),
                pl.BlockSpec((TOP_K, tm), lambda i, offs: (0, i)),
                pl.BlockSpec((tm, D_MODEL), lambda i, offs: (i, 0)),
                pl.BlockSpec(memory_space=pl.ANY),
                _resident((1, D_MODEL), lambda i, offs: (0, 0)),
                pl.BlockSpec((None, 1, D_MODEL),
                             lambda i, offs: (mod_base + (i // tiles_per_seq) * N_MOD + 5, 0, 0)),
            ],
            out_specs=pl.BlockSpec((tm, D_MODEL), lambda i, offs: (i, 0)),
            scratch_shapes=[
                pltpu.VMEM((TOP_K, tm * ROW_SUB, LANES), F32),
                pltpu.SemaphoreType.DMA(()),
            ],
        ),
        out_shape=jax.ShapeDtypeStruct((n, D_MODEL), F32),
        compiler_params=_params(1, n_arbitrary=1),
        name="l1_combine",
    )(offs, idx, rank, gates, x2d, ys, g_post, mod_r)


def _moe(x2d, mod_r, mod_base, g_pre, g_post, router_w, wg, wu, wd, seq):
    n = x2d.shape[0]
    hp, idx, gates, rank, counts = _route(x2d, mod_r, mod_base, g_pre, router_w, seq)
    n_tiles = TOP_K * n // EXPERT_TILE + N_EXPERTS
    cnt = counts[:, 0].astype(jnp.int32)
    tiles = (cnt + EXPERT_TILE - 1) // EXPERT_TILE
    tile_end = jnp.cumsum(tiles)
    offs = (tile_end - tiles) * EXPERT_TILE
    n_valid = tile_end[-1:]
    tile_ids = jnp.arange(n_tiles, dtype=jnp.int32)
    tile_expert = jnp.minimum(jnp.sum(tile_ids[:, None] >= tile_end[None, :], axis=1), N_EXPERTS - 1)
    tile_expert = jnp.where(tile_ids < n_valid[0], tile_expert, tile_expert[jnp.maximum(n_valid[0] - 1, 0)])
    xs = _dispatch(offs, idx, rank, hp, n_tiles * EXPERT_TILE)
    ys = _expert_ffn(tile_expert.astype(jnp.int32), n_valid, xs, wg, wu, wd)
    return _combine(offs, idx, rank, gates, x2d, ys, mod_r, mod_base, g_post, seq)


def kernel(x, c, mod_w, mod_b, norm_g, l0_w_in, l0_w_out, l0_ffn_w_gate, l0_ffn_w_up, l0_ffn_w_down,
           l1_w_in, l1_q_norm_g, l1_k_norm_g, l1_w_out, l1_router_w, l1_exp_w_gate, l1_exp_w_up,
           l1_exp_w_down):
    batch, seq, d = x.shape
    assert d == D_MODEL and seq % 2048 == 0
    n = batch * seq
    x2d = x.reshape(n, d)
    mod = _mod_vectors(c, mod_w, mod_b)
    mod_r = mod.reshape(mod.shape[0] * batch * N_MOD, 1, d)
    gains = norm_g.reshape(norm_g.shape[0], norm_g.shape[1], 1, d)

    base0 = 0
    gw = 3 * d
    o_list, lse_list = [], []
    for gi, (_, dil) in enumerate(DILATED_GROUPS):
        w_g = l0_w_in[:, gi * gw:(gi + 1) * gw].astype(BF16)
        qkv = _proj0(x2d, mod_r, base0, gains[0, 0], w_g, dil, batch, seq)
        o_g, lse_g = _attn0(qkv, dil)
        o_list.append(o_g)
        lse_list.append(lse_g)
    x2d = _merge_out(o_list, lse_list, x2d, mod_r, base0, l0_w_out.astype(BF16), gains[0, 1], batch, seq)
    x2d = _ffn(x2d, mod_r, base0, gains[0, 2], gains[0, 3], l0_ffn_w_gate.astype(BF16),
               l0_ffn_w_up.astype(BF16), l0_ffn_w_down.astype(BF16), seq)

    base1 = batch * N_MOD
    qt, kd, vt = _proj1(x2d, mod_r, base1, gains[1, 0], l1_w_in.astype(BF16), l1_q_norm_g, l1_k_norm_g,
                        batch, seq)
    o = _attn1(qt, kd, vt, batch, seq)
    x2d = _out_proj(o, x2d, mod_r, base1, l1_w_out.astype(BF16), gains[1, 1], seq)
    x2d = _moe(x2d, mod_r, base1, gains[1, 2], gains[1, 3], l1_router_w, l1_exp_w_gate.astype(BF16),
               l1_exp_w_up.astype(BF16), l1_exp_w_down.astype(BF16), seq)
    return x2d.reshape(batch, seq, d)
```
